```python
import math
import jax, jax.numpy as jnp
from jax import lax
import numpy as np

D_MODEL = 1024
BATCH = 8
SEQ = 4096
DEPTH = 2
DEC_BATCH = 128
DEC_SEQ = 1
PAST_LEN = 16384
PAGE_SIZE = 128

N_EVEN = (DEPTH + 1) // 2
N_ODD = DEPTH // 2
EPS = 1e-6
D_A = D_MODEL
CONV_A = 31
D_B = D_MODEL
H_B = 4
DH_B = D_B // H_B
CONV_B = 4
CHUNK_B = 64
HD_C = 64
H_C = D_MODEL // HD_C
N_KV_C = H_C // 8
G_C = H_C // N_KV_C
WINDOW = 128
ROT_DIM = HD_C // 4
ROPE_THETA = 500000.0
P_EVEN = 3 * D_A + 4 * D_B + 2 * H_B
SPLIT_EVEN = [D_A, 2 * D_A, 3 * D_A, 3 * D_A + D_B, 3 * D_A + 2 * D_B, 3 * D_A + 3 * D_B, 3 * D_A + 4 * D_B]
P_ODD = 2 * H_C * HD_C + 2 * N_KV_C * HD_C
SPLIT_ODD = [H_C * HD_C, H_C * HD_C + N_KV_C * HD_C, H_C * HD_C + 2 * N_KV_C * HD_C]

kernel_name = 'hybrid_conv_mlstm_swa_decoder_step'


def rmsnorm(x, g):
    xf = x.astype(jnp.float32)
    y = xf * lax.rsqrt(jnp.mean(xf * xf, axis=-1, keepdims=True) + EPS)
    return (y * g.astype(jnp.float32)).astype(x.dtype)


def layernorm(x, g, b):
    xf = x.astype(jnp.float32)
    mu = jnp.mean(xf, axis=-1, keepdims=True)
    var = jnp.mean(jnp.square(xf - mu), axis=-1, keepdims=True)
    y = (xf - mu) * lax.rsqrt(var + EPS) * g.astype(jnp.float32) + b.astype(jnp.float32)
    return y.astype(x.dtype)


def causal_dwconv(u, buf, w, b):
    width, ch = w.shape
    full = jnp.concatenate([buf.astype(u.dtype), u], axis=1)
    out = lax.conv_general_dilated(full, w[:, None, :].astype(u.dtype), window_strides=(1,), padding='VALID',
                                   dimension_numbers=('NWC', 'WIO', 'NWC'), feature_group_count=ch)
    return out + b.astype(u.dtype), full[:, full.shape[1] - (width - 1):]


def rope_partial(x, pos):
    half = ROT_DIM // 2
    inv = jnp.power(ROPE_THETA, -jnp.arange(half, dtype=jnp.float32) * (2.0 / ROT_DIM))
    ang = pos.astype(jnp.float32)[:, None] * inv[None, :]
    cos = jnp.cos(ang)[None, :, None, :]
    sin = jnp.sin(ang)[None, :, None, :]
    xr = x[..., :ROT_DIM].astype(jnp.float32)
    x1, x2 = xr[..., :half], xr[..., half:]
    rot = jnp.concatenate([x1 * cos - x2 * sin, x2 * cos + x1 * sin], axis=-1).astype(x.dtype)
    return jnp.concatenate([rot, x[..., ROT_DIM:]], axis=-1)


def sink_weights(s, valid, sinks):
    sk = sinks.astype(jnp.float32).reshape(N_KV_C, G_C, 1, 1)
    s = jnp.where(valid, s, -jnp.inf)
    m = jnp.maximum(jnp.max(s, axis=-1, keepdims=True), sk)
    p = jnp.exp(s - m)
    return p / (jnp.sum(p, axis=-1, keepdims=True) + jnp.exp(sk - m))


def swa_prompt(q, k, v, sinks):
    bsz, t_len = q.shape[:2]
    nb = t_len // WINDOW
    qb = q.reshape(bsz, nb, WINDOW, N_KV_C, G_C, HD_C)
    kb = k.reshape(bsz, nb, WINDOW, N_KV_C, HD_C)
    vb = v.reshape(bsz, nb, WINDOW, N_KV_C, HD_C)
    pad = ((0, 0), (1, 0), (0, 0), (0, 0), (0, 0))
    kk = jnp.concatenate([jnp.pad(kb, pad)[:, :-1], kb], axis=2)
    vv = jnp.concatenate([jnp.pad(vb, pad)[:, :-1], vb], axis=2)
    s = jnp.einsum('bnqhgd,bnkhd->bnhgqk', qb, kk, preferred_element_type=jnp.float32) * (HD_C ** -0.5)
    blk = jnp.arange(nb)[:, None]
    tpos = blk * WINDOW + jnp.arange(WINDOW)[None, :]
    spos = (blk - 1) * WINDOW + jnp.arange(2 * WINDOW)[None, :]
    tq, sk_ = tpos[:, :, None], spos[:, None, :]
    valid = (sk_ <= tq) & (sk_ >= tq - WINDOW) & (sk_ >= 0)
    p = sink_weights(s, valid[None, :, None, None], sinks)
    o = jnp.einsum('bnhgqk,bnkhd->bnqhgd', p.astype(vv.dtype), vv)
    return o.reshape(bsz, t_len, H_C * HD_C)


def swa_sample(q, k, v, k_buf, v_buf, sinks):
    bsz, l_len = q.shape[:2]
    nbuf = k_buf.shape[1]
    kk = jnp.concatenate([k_buf.astype(k.dtype), k], axis=1)
    vv = jnp.concatenate([v_buf.astype(v.dtype), v], axis=1)
    qg = q.reshape(bsz, l_len, N_KV_C, G_C, HD_C)
    s = jnp.einsum('bqhgd,bkhd->bhgqk', qg, kk, preferred_element_type=jnp.float32) * (HD_C ** -0.5)
    tq = (PAST_LEN + jnp.arange(l_len))[:, None]
    sk_ = (PAST_LEN - nbuf + jnp.arange(nbuf + l_len))[None, :]
    valid = (sk_ <= tq) & (sk_ >= tq - WINDOW) & (sk_ >= 0)
    p = sink_weights(s, valid, sinks)
    o = jnp.einsum('bhgqk,bkhd->bqhgd', p.astype(vv.dtype), vv)
    return o.reshape(bsz, l_len, H_C * HD_C), kk[:, -nbuf:], vv[:, -nbuf:]


def mlstm_chunkwise(q, k, v, li, lf, c0, n0, m0, chunk):
    bsz, l_len, nh, dh = q.shape
    nc = l_len // chunk
    f32 = jnp.float32

    def to_chunks(t):
        return jnp.moveaxis(t.astype(f32).reshape((bsz, nc, chunk) + t.shape[2:]), 1, 0)

    xs = (to_chunks(q), to_chunks(k), to_chunks(v), to_chunks(li), to_chunks(lf))
    causal = jnp.tril(jnp.ones((chunk, chunk), dtype=bool))[None, :, :, None]

    def step(carry, inp):
        c, n, m = carry
        qc, kc, vc, ic, fc = inp
        b = jnp.cumsum(fc, axis=1)
        inter = b + m[:, None]
        dmat = jnp.where(causal, b[:, :, None] - b[:, None] + ic[:, None], -jnp.inf)
        mt = jnp.maximum(inter, jnp.max(dmat, axis=2))
        w = jnp.einsum('bthd,bshd->btsh', qc, kc) * jnp.exp(dmat - mt[:, :, None])
        a_in = jnp.exp(inter - mt)
        num = jnp.einsum('btsh,bshd->bthd', w, vc) + a_in[..., None] * jnp.einsum('bhkv,bthk->bthv', c, qc)
        den = jnp.sum(w, axis=2) + a_in * jnp.einsum('bhk,bthk->bth', n, qc)
        hc = num / jnp.maximum(jnp.abs(den), jnp.exp(-mt))[..., None]
        bl = b[:, -1]
        wl = bl[:, None] - b + ic
        m_new = jnp.maximum(bl + m, jnp.max(wl, axis=1))
        g0 = jnp.exp(bl + m - m_new)
        ws = jnp.exp(wl - m_new[:, None])
        c_new = g0[..., None, None] * c + jnp.einsum('bsh,bshk,bshv->bhkv', ws, kc, vc)
        n_new = g0[..., None] * n + jnp.einsum('bsh,bshk->bhk', ws, kc)
        return (c_new, n_new, m_new), hc

    (c1, n1, m1), hs = lax.scan(step, (c0.astype(f32), n0.astype(f32), m0.astype(f32)), xs)
    h = jnp.moveaxis(hs, 0, 1).reshape(bsz, l_len, nh, dh)
    return h, c1, n1, m1


def even_layer(x, ew, conv_a_buf, conv_b_buf, c0, n0, m0, chunk):
    (g_pre, g_post, w_in, cwa, cba, lng, lnb, cwb, cbb, wq, wk, bi, bf, hng, skip, w_out) = ew
    bsz, l_len, _ = x.shape
    proj = rmsnorm(x, g_pre) @ w_in
    a_val, a_gate, a_z, u, v, o_pre, z_b, gates = jnp.split(proj, SPLIT_EVEN, axis=-1)
    a = a_val * jax.nn.sigmoid(a_gate)
    a, new_ca = causal_dwconv(a, conv_a_buf, cwa, cba)
    a = jax.nn.silu(layernorm(a, lng, lnb)) * jax.nn.silu(a_z)
    uc, new_cb = causal_dwconv(u, conv_b_buf, cwb, cbb)
    uc = jax.nn.silu(uc)
    uh = uc.reshape(bsz, l_len, H_B, DH_B)
    q = jnp.einsum('blhd,hde->blhe', uh, wq)
    k = jnp.einsum('blhd,hde->blhe', uh, wk) * (DH_B ** -0.5)
    vh = v.reshape(bsz, l_len, H_B, DH_B)
    li = (gates[..., :H_B] + bi).astype(jnp.float32)
    lf = jax.nn.log_sigmoid((gates[..., H_B:] + bf).astype(jnp.float32))
    hc, c1, n1, m1 = mlstm_chunkwise(q, k, vh, li, lf, c0, n0, m0, chunk)
    hc = jax.nn.sigmoid(o_pre).reshape(bsz, l_len, H_B, DH_B) * hc.astype(x.dtype)
    hb = rmsnorm(hc, hng.reshape(H_B, DH_B)).reshape(bsz, l_len, D_B)
    hb = (hb + skip * uc) * jax.nn.silu(z_b)
    y = jnp.concatenate([a, hb], axis=-1) @ w_out
    return x + rmsnorm(y, g_post), new_ca, new_cb, c1, n1, m1


def odd_layer(x, ow, pos, kv_buf):
    (g_pre, g_post, w_in, sinks, w_out) = ow
    bsz, l_len, _ = x.shape
    proj = rmsnorm(x, g_pre) @ w_in
    q, k, v, z = jnp.split(proj, SPLIT_ODD, axis=-1)
    q = rope_partial(q.reshape(bsz, l_len, H_C, HD_C), pos)
    k = rope_partial(k.reshape(bsz, l_len, N_KV_C, HD_C), pos)
    v = v.reshape(bsz, l_len, N_KV_C, HD_C)
    if kv_buf is None:
        o = swa_prompt(q, k, v, sinks)
        new_k, new_v = k[:, -WINDOW:], v[:, -WINDOW:]
    else:
        o, new_k, new_v = swa_sample(q, k, v, kv_buf[0], kv_buf[1], sinks)
    y = (o * jax.nn.silu(z)) @ w_out
    return x + rmsnorm(y, g_post), new_k, new_v


def setup_inputs(seed: int = 0) -> dict:
    key = jax.random.key(seed)
    ks = iter(jax.random.split(key, 40))

    def nrm(shape, scale):
        return jax.random.normal(next(ks), shape, jnp.float32) * scale

    def gain(shape):
        return 1.0 + nrm(shape, 0.02)

    win_rows = min(WINDOW, PAST_LEN)
    return {
        'x_prompt': nrm((BATCH, SEQ, D_MODEL), 1.0),
        'x_sample': nrm((DEC_BATCH, DEC_SEQ, D_MODEL), 1.0),
        'state_conv_a': nrm((N_EVEN, DEC_BATCH, CONV_A - 1, D_A), 0.5),
        'state_conv_b': nrm((N_EVEN, DEC_BATCH, CONV_B - 1, D_B), 1.0),
        'state_mlstm_c': nrm((N_EVEN, DEC_BATCH, H_B, DH_B, DH_B), 0.05),
        'state_mlstm_n': nrm((N_EVEN, DEC_BATCH, H_B, DH_B), 0.5),
        'state_mlstm_m': nrm((N_EVEN, DEC_BATCH, H_B), 1.0),
        'cache_k_win': nrm((N_ODD, DEC_BATCH, win_rows, N_KV_C, HD_C), 1.0),
        'cache_v_win': nrm((N_ODD, DEC_BATCH, win_rows, N_KV_C, HD_C), 1.0),
        'norm_pre_e': gain((N_EVEN, D_MODEL)),
        'norm_post_e': gain((N_EVEN, D_MODEL)),
        'w_in_e': nrm((N_EVEN, D_MODEL, P_EVEN), D_MODEL ** -0.5),
        'conv_a_w': nrm((N_EVEN, CONV_A, D_A), CONV_A ** -0.5),
        'conv_a_b': nrm((N_EVEN, D_A), 0.02),
        'ln_a_g': gain((N_EVEN, D_A)),
        'ln_a_b': nrm((N_EVEN, D_A), 0.02),
        'conv_b_w': nrm((N_EVEN, CONV_B, D_B), CONV_B ** -0.5),
        'conv_b_b': nrm((N_EVEN, D_B), 0.02),
        'wq_b': nrm((N_EVEN, H_B, DH_B, DH_B), DH_B ** -0.5),
        'wk_b': nrm((N_EVEN, H_B, DH_B, DH_B), DH_B ** -0.5),
        'b_i': nrm((N_EVEN, H_B), 0.1),
        'b_f': jnp.linspace(3.0, 6.0, H_B, dtype=jnp.float32)[None, :] + nrm((N_EVEN, H_B), 0.1),
        'headnorm_b': gain((N_EVEN, D_B)),
        'skip_b': gain((N_EVEN, D_B)),
        'w_out_e': nrm((N_EVEN, D_A + D_B, D_MODEL), (D_A + D_B) ** -0.5),
        'norm_pre_o': gain((N_ODD, D_MODEL)),
        'norm_post_o': gain((N_ODD, D_MODEL)),
        'w_in_o': nrm((N_ODD, D_MODEL, P_ODD), D_MODEL ** -0.5),
        'sinks': nrm((N_ODD, H_C), 0.5),
        'w_out_o': nrm((N_ODD, H_C * HD_C, D_MODEL), (H_C * HD_C) ** -0.5),
    }


def reference(x_prompt, x_sample, state_conv_a, state_conv_b, state_mlstm_c, state_mlstm_n, state_mlstm_m,
              cache_k_win, cache_v_win, norm_pre_e, norm_post_e, w_in_e, conv_a_w, conv_a_b, ln_a_g, ln_a_b,
              conv_b_w, conv_b_b, wq_b, wk_b, b_i, b_f, headnorm_b, skip_b, w_out_e,
              norm_pre_o, norm_post_o, w_in_o, sinks, w_out_o):
    bsz, seq_len, _ = x_prompt.shape
    dec_len = x_sample.shape[1]
    dt = x_prompt.dtype
    f32 = jnp.float32
    pos_p = jnp.arange(seq_len, dtype=jnp.int32)
    pos_s = PAST_LEN + jnp.arange(dec_len, dtype=jnp.int32)
    xp, xs = x_prompt, x_sample
    pa, pb, pc, pn, pm, pk, pv = [], [], [], [], [], [], []
    sa, sb, sc, sn, sm, sk, sv = [], [], [], [], [], [], []
    for layer in range(DEPTH):
        j = layer // 2
        if layer % 2 == 0:
            ew = (norm_pre_e[j], norm_post_e[j], w_in_e[j], conv_a_w[j], conv_a_b[j], ln_a_g[j], ln_a_b[j],
                  conv_b_w[j], conv_b_b[j], wq_b[j], wk_b[j], b_i[j], b_f[j], headnorm_b[j], skip_b[j], w_out_e[j])
            xp, a1, b1, c1, n1, m1 = even_layer(
                xp, ew, jnp.zeros((bsz, CONV_A - 1, D_A), dt), jnp.zeros((bsz, CONV_B - 1, D_B), dt),
                jnp.zeros((bsz, H_B, DH_B, DH_B), f32), jnp.zeros((bsz, H_B, DH_B), f32),
                jnp.zeros((bsz, H_B), f32), CHUNK_B)
            xs, a2, b2, c2, n2, m2 = even_layer(
                xs, ew, state_conv_a[j], state_conv_b[j], state_mlstm_c[j], state_mlstm_n[j],
                state_mlstm_m[j], dec_len)
            pa.append(a1); pb.append(b1); pc.append(c1); pn.append(n1); pm.append(m1)
            sa.append(a2); sb.append(b2); sc.append(c2); sn.append(n2); sm.append(m2)
        else:
            ow = (norm_pre_o[j], norm_post_o[j], w_in_o[j], sinks[j], w_out_o[j])
            xp, k1, v1 = odd_layer(xp, ow, pos_p, None)
            xs, k2, v2 = odd_layer(xs, ow, pos_s, (cache_k_win[j], cache_v_win[j]))
            pk.append(k1); pv.append(v1)
            sk.append(k2); sv.append(v2)
    conv_a_p, conv_b_p = jnp.stack(pa), jnp.stack(pb)
    mlstm_c_p, mlstm_n_p, mlstm_m_p = jnp.stack(pc), jnp.stack(pn), jnp.stack(pm)
    k_win_p, v_win_p = jnp.stack(pk), jnp.stack(pv)
    conv_a_s, conv_b_s = jnp.stack(sa), jnp.stack(sb)
    mlstm_c_s, mlstm_n_s, mlstm_m_s = jnp.stack(sc), jnp.stack(sn), jnp.stack(sm)
    k_win_s, v_win_s = jnp.stack(sk), jnp.stack(sv)
    return (xp, xs, conv_a_p, conv_b_p, mlstm_c_p, mlstm_n_p, mlstm_m_p, k_win_p, v_win_p,
            conv_a_s, conv_b_s, mlstm_c_s, mlstm_n_s, mlstm_m_s, k_win_s, v_win_s)
```

```python
import functools

import jax
import jax.numpy as jnp
from jax import lax
from jax.experimental import pallas as pl
from jax.experimental.pallas import tpu as pltpu

F32 = jnp.float32
BF16 = jnp.bfloat16

D_MODEL = 1024
EPS = 1e-6
PAST_LEN = 16384
CONV_A = 31
CONV_B = 4
H_B = 4
DH_B = D_MODEL // H_B
HD_C = 64
H_C = D_MODEL // HD_C
N_KV_C = 2
G_C = H_C // N_KV_C
WINDOW = 128
ROT_DIM = HD_C // 4
ROPE_THETA = 500000.0
K_SCALE_B = DH_B ** -0.5
Q_SCALE_C = HD_C ** -0.5

LANES = 128
NG = D_MODEL // LANES
STRIDE = 4
TILE_EVEN = 256
TILE_ODD = 256
ROWS = 16
SAMPLE_BLOCK = 8
VMEM_LIMIT = 56 * 1024 * 1024

NT_DIMS = (((1,), (1,)), ((), ()))


def _dot(a, b):
    return jnp.dot(a, b, preferred_element_type=F32)


def _dot_nt(a, b):
    return lax.dot_general(a, b, NT_DIMS, preferred_element_type=F32)


def _sigmoid(x):
    return 1.0 / (1.0 + jnp.exp(-x))


def _silu(x):
    return x * _sigmoid(x)


def _log_sigmoid(x):
    return jnp.minimum(x, 0.0) - jnp.log1p(jnp.exp(-jnp.abs(x)))


def _rms(x, g):
    ms = jnp.mean(x * x, axis=-1, keepdims=True)
    return x * lax.rsqrt(ms + EPS) * g


def _layernorm(x, g, b):
    mu = jnp.mean(x, axis=-1, keepdims=True)
    xc = x - mu
    var = jnp.mean(xc * xc, axis=-1, keepdims=True)
    return xc * lax.rsqrt(var + EPS) * g + b


def _row_loop(total, rows, body):
    def step(i, carry):
        body(pl.multiple_of(i * rows, rows))
        return carry
    lax.fori_loop(0, total // rows, step, 0)


def _split3(a):
    hi = a.astype(BF16)
    r1 = a - hi.astype(F32)
    mid = r1.astype(BF16)
    lo = (r1 - mid.astype(F32)).astype(BF16)
    return hi, mid, lo


def _conv_strided(src, w_ref, dst, taps, off, total):
    chunk_rows = 8 * STRIDE

    def chunk(c, carry):
        base = pl.multiple_of(c * chunk_rows, chunk_rows)
        for g in range(NG):
            gs = slice(g * LANES, (g + 1) * LANES)
            accs = [None] * STRIDE
            for j in range(taps):
                wj = jnp.broadcast_to(w_ref[j:j + 1, gs], (8, LANES))
                for p in range(STRIDE):
                    v = src[g, pl.ds(base + (p + off + j), 8, stride=STRIDE), :] * wj
                    accs[p] = v if j == 0 else accs[p] + v
            for p in range(STRIDE):
                dst[g, pl.ds(base + p, 8, stride=STRIDE), :] = accs[p]
        return carry
    lax.fori_loop(0, total // chunk_rows, chunk, 0)


def _even_prompt_body(T, NT, x_ref, gpre_ref, gpost_ref, win_ref, wg_ref, wgt_ref, gbr_ref, gbc_ref,
                      cwa_ref, cba_ref, lng_ref, lnb_ref, cwb_ref, cbb_ref, wq_ref, wk_ref, wkt_ref,
                      hng_ref, skip_ref, wout_ref,
                      y_ref, ca_ref, cb_ref, c_ref, n_ref, m_ref,
                      hn_s, p_s, abuf, ubuf, cbuf, uc_s, ucb_s, ycat_s, cst_s, nst_s, mst_s):
    D = D_MODEL
    R = ROWS
    t = pl.program_id(1)

    @pl.when(t == 0)
    def _():
        abuf[:, 0:32, :] = jnp.zeros((NG, 32, LANES), F32)
        ubuf[:, 0:8, :] = jnp.zeros((NG, 8, LANES), F32)
        cst_s[...] = jnp.zeros(cst_s.shape, F32)
        nst_s[...] = jnp.zeros(nst_s.shape, F32)
        mst_s[...] = jnp.zeros(mst_s.shape, F32)

    def p_norm(r0):
        sl = pl.ds(r0, R)
        hn_s[sl, :] = _rms(x_ref[0, sl, :], gpre_ref[...]).astype(BF16)
    _row_loop(T, R, p_norm)

    hn = hn_s[...]

    def proj(p):
        return _dot(hn, win_ref[:, p * D:(p + 1) * D])
    p_s[0] = proj(0)
    p_s[1] = proj(1)
    p_s[2] = proj(2)
    u_new = proj(3)
    for g in range(NG):
        ubuf[g, 8:8 + T, :] = u_new[:, g * LANES:(g + 1) * LANES]
    p_s[3] = proj(4)
    p_s[4] = proj(5)
    p_s[5] = proj(6)
    gc = _dot(hn, wg_ref[...]) + gbr_ref[...]
    gr = _dot_nt(wgt_ref[...], hn) + gbc_ref[...]

    row = lax.broadcasted_iota(jnp.int32, (T, T), 0)
    col = lax.broadcasted_iota(jnp.int32, (T, T), 1)
    causal = row >= col
    tril = jnp.where(causal, 1.0, 0.0).astype(BF16)
    triu = jnp.where(row <= col, 1.0, 0.0).astype(BF16)
    lf_c = _log_sigmoid(gc)
    lf_r = _log_sigmoid(gr)
    ch, cm, cl = _split3(lf_c)
    b_c = _dot(tril, ch) + _dot(tril, cm) + _dot(tril, cl)
    rh, rm, rl = _split3(lf_r)
    b_r = _dot(rh, triu) + _dot(rm, triu) + _dot(rl, triu)

    def p_glu(r0):
        sl = pl.ds(r0, R)
        for g in range(NG):
            gs = slice(g * LANES, (g + 1) * LANES)
            abuf[g, pl.ds(r0 + 32, R), :] = p_s[0, sl, gs] * _sigmoid(p_s[1, sl, gs])
    _row_loop(T, R, p_glu)

    _conv_strided(abuf, cwa_ref, cbuf, CONV_A, 32 - (CONV_A - 1), T)

    def p_ln(r0):
        sl = pl.ds(r0, R)
        xs = [cbuf[g, sl, :] + cba_ref[:, g * LANES:(g + 1) * LANES] for g in range(NG)]
        tot = xs[0]
        for g in range(1, NG):
            tot = tot + xs[g]
        mu = jnp.sum(tot, axis=-1, keepdims=True) * (1.0 / D)
        xc = [x - mu for x in xs]
        sq = xc[0] * xc[0]
        for g in range(1, NG):
            sq = sq + xc[g] * xc[g]
        rs = lax.rsqrt(jnp.sum(sq, axis=-1, keepdims=True) * (1.0 / D) + EPS)
        for g in range(NG):
            gs = slice(g * LANES, (g + 1) * LANES)
            yv = xc[g] * rs * lng_ref[:, gs] + lnb_ref[:, gs]
            ycat_s[sl, gs] = (_silu(yv) * _silu(p_s[2, sl, gs])).astype(BF16)
    _row_loop(T, R, p_ln)

    _conv_strided(ubuf, cwb_ref, cbuf, CONV_B, 8 - (CONV_B - 1), T)

    def p_uc(r0):
        sl = pl.ds(r0, R)
        for g in range(NG):
            gs = slice(g * LANES, (g + 1) * LANES)
            uc = _silu(cbuf[g, sl, :] + cbb_ref[:, gs])
            uc_s[sl, gs] = uc
            ucb_s[sl, gs] = uc.astype(BF16)
    _row_loop(T, R, p_uc)

    for h in range(H_B):
        hs = slice(h * DH_B, (h + 1) * DH_B)
        ub = ucb_s[:, hs]
        q = _dot(ub, wq_ref[h])
        k = _dot(ub, wk_ref[h]) * K_SCALE_B
        kt = _dot_nt(wkt_ref[h], ub) * K_SCALE_B
        qb = q.astype(BF16)
        s = _dot(qb, kt.astype(BF16))
        bc = b_c[:, 4 + h:5 + h]
        br = b_r[4 + h:5 + h, :]
        lic = gc[:, h:h + 1]
        lir = gr[h:h + 1, :]
        m_h = mst_s[h:h + 1, 0:1]
        dm = jnp.where(causal, bc - br + lir, -jnp.inf)
        inter = bc + m_h
        mt = jnp.maximum(inter, jnp.max(dm, axis=1, keepdims=True))
        w = s * jnp.exp(dm - mt)
        a_in = jnp.exp(inter - mt)
        vb = p_s[3, :, hs].astype(BF16)
        cmat = cst_s[h]
        nrow = nst_s[h:h + 1, :]
        num = _dot(w.astype(BF16), vb) + a_in * _dot(qb, cmat.astype(BF16))
        den = jnp.sum(w, axis=1, keepdims=True) + a_in * jnp.sum(q * nrow, axis=1, keepdims=True)
        p_s[0, :, hs] = num / jnp.maximum(jnp.abs(den), jnp.exp(-mt))
        bl = bc[T - 1:T, :]
        wl_r = bl - br + lir
        wl_c = bl - bc + lic
        m_new = jnp.maximum(bl + m_h, jnp.max(wl_r, axis=1, keepdims=True))
        g0 = jnp.exp(bl + m_h - m_new)
        ws_r = jnp.exp(wl_r - m_new)
        ws_c = jnp.exp(wl_c - m_new)
        cst_s[h] = g0 * cmat + _dot((kt * ws_r).astype(BF16), vb)
        nst_s[h:h + 1, :] = g0 * nrow + jnp.sum(ws_c * k, axis=0, keepdims=True)
        mst_s[h:h + 1, :] = jnp.broadcast_to(m_new, (1, LANES))

    def p_hb(r0):
        sl = pl.ds(r0, R)
        for h in range(H_B):
            hs = slice(h * DH_B, (h + 1) * DH_B)
            o = _sigmoid(p_s[4, sl, hs]) * p_s[0, sl, hs]
            hb = _rms(o, hng_ref[:, hs])
            hb = (hb + skip_ref[:, hs] * uc_s[sl, hs]) * _silu(p_s[5, sl, hs])
            ycat_s[sl, D + h * DH_B:D + (h + 1) * DH_B] = hb.astype(BF16)
    _row_loop(T, R, p_hb)

    p_s[1] = _dot(ycat_s[...], wout_ref[...])

    def p_out(r0):
        sl = pl.ds(r0, R)
        y_ref[0, sl, :] = x_ref[0, sl, :] + _rms(p_s[1, sl, :], gpost_ref[...])
    _row_loop(T, R, p_out)

    @pl.when(t == NT - 1)
    def _():
        for g in range(NG):
            ca_ref[0, :, g * LANES:(g + 1) * LANES] = abuf[g, T:T + 32, :]
            cb_ref[0, :, g * LANES:(g + 1) * LANES] = ubuf[g, T:T + 8, :]
        c_ref[0] = cst_s[...]
        n_ref[0] = nst_s[0:H_B, :]
        m_ref[0] = mst_s[...]

    abuf[:, 0:32, :] = abuf[:, T:T + 32, :]
    ubuf[:, 0:8, :] = ubuf[:, T:T + 8, :]


def _const_spec(shape):
    nd = len(shape)
    return pl.BlockSpec(shape, lambda *_: (0,) * nd, pipeline_mode=pl.Buffered(1))


def _even_prompt(x, ew, T):
    B, L, D = x.shape
    NT = L // T
    consts = (ew['gpre'], ew['gpost'], ew['win'], ew['wg'], ew['wgt'], ew['gbr'], ew['gbc'],
              ew['cwa'], ew['cba'], ew['lng'], ew['lnb'], ew['cwb'], ew['cbb'], ew['wq'], ew['wk'],
              ew['wkt'], ew['hng'], ew['skip'], ew['wout'])
    in_specs = [pl.BlockSpec((1, T, D), lambda b, t: (b, t, 0))] + [_const_spec(c.shape) for c in consts]
    out_shape = (
        jax.ShapeDtypeStruct((B, L, D), F32),
        jax.ShapeDtypeStruct((B, 32, D), F32),
        jax.ShapeDtypeStruct((B, 8, D), F32),
        jax.ShapeDtypeStruct((B, H_B, DH_B, DH_B), F32),
        jax.ShapeDtypeStruct((B, H_B, DH_B), F32),
        jax.ShapeDtypeStruct((B, 8, LANES), F32),
    )
    out_specs = (
        pl.BlockSpec((1, T, D), lambda b, t: (b, t, 0)),
        pl.BlockSpec((1, 32, D), lambda b, t: (b, 0, 0)),
        pl.BlockSpec((1, 8, D), lambda b, t: (b, 0, 0)),
        pl.BlockSpec((1, H_B, DH_B, DH_B), lambda b, t: (b, 0, 0, 0)),
        pl.BlockSpec((1, H_B, DH_B), lambda b, t: (b, 0, 0)),
        pl.BlockSpec((1, 8, LANES), lambda b, t: (b, 0, 0)),
    )
    scratch = [
        pltpu.VMEM((T, D), BF16),
        pltpu.VMEM((6, T, D), F32),
        pltpu.VMEM((NG, T + 32, LANES), F32),
        pltpu.VMEM((NG, T + 8, LANES), F32),
        pltpu.VMEM((NG, T, LANES), F32),
        pltpu.VMEM((T, D), F32),
        pltpu.VMEM((T, D), BF16),
        pltpu.VMEM((T, 2 * D), BF16),
        pltpu.VMEM((H_B, DH_B, DH_B), F32),
        pltpu.VMEM((8, DH_B), F32),
        pltpu.VMEM((8, LANES), F32),
    ]
    return pl.pallas_call(
        functools.partial(_even_prompt_body, T, NT),
        grid=(B, NT),
        in_specs=in_specs,
        out_specs=out_specs,
        out_shape=out_shape,
        scratch_shapes=scratch,
        compiler_params=pltpu.CompilerParams(
            dimension_semantics=("arbitrary", "arbitrary"), vmem_limit_bytes=VMEM_LIMIT),
        name="even_prompt",
    )(x, *consts)


def _rope_slab(x, cos, sin, lane_lo):
    partner = jnp.where(lane_lo, pltpu.roll(x, LANES - ROT_DIM // 2, 1), pltpu.roll(x, ROT_DIM // 2, 1))
    return x * cos + partner * sin


def _odd_prompt_body(T, NT, x_ref, gpre_ref, gpost_ref, wq_ref, wkv_ref, wz_ref, wo_ref, sinks_ref,
                     cq_ref, sq_ref, ck_ref, sk_ref,
                     y_ref, kw_ref, vw_ref,
                     hn_s, pq_s, pz_s, pkv_s, qr_s, kr_s, kf_s, vf_s, o_s, g_s):
    D = D_MODEL
    R = ROWS
    W = WINDOW
    t = pl.program_id(1)

    @pl.when(t == 0)
    def _():
        kf_s[:, 0:W, :] = jnp.zeros((4, W, LANES), BF16)
        vf_s[:, 0:W, :] = jnp.zeros((4, W, LANES), BF16)

    def p_norm(r0):
        sl = pl.ds(r0, R)
        hn_s[sl, :] = _rms(x_ref[0, sl, :], gpre_ref[...]).astype(BF16)
    _row_loop(T, R, p_norm)

    hn = hn_s[...]
    pq_s[...] = _dot(hn, wq_ref[...])
    pkv_s[...] = _dot(hn, wkv_ref[...])
    pz_s[...] = _dot(hn, wz_ref[...])

    def p_rope(r0):
        sl = pl.ds(r0, R)
        lane = lax.broadcasted_iota(jnp.int32, (R, LANES), 1)
        lane_lo = (lane % HD_C) < (ROT_DIM // 2)
        head_lo = lane < HD_C
        cq, sq = cq_ref[sl, :], sq_ref[sl, :]
        for j in range(D // LANES):
            ls = slice(j * LANES, (j + 1) * LANES)
            qr_s[sl, ls] = _rope_slab(pq_s[sl, ls], cq, sq, lane_lo).astype(BF16)
        kr = _rope_slab(pkv_s[sl, 0:LANES], ck_ref[sl, :], sk_ref[sl, :], lane_lo)
        kr_s[sl, :] = kr
        krr = pltpu.roll(kr, HD_C, 1)
        vr = pkv_s[sl, LANES:2 * LANES]
        vrr = pltpu.roll(vr, HD_C, 1)
        dst = pl.ds(r0 + W, R)
        zero = jnp.zeros_like(kr)
        kf_s[0, dst, :] = jnp.where(head_lo, kr, zero).astype(BF16)
        kf_s[1, dst, :] = jnp.where(head_lo, zero, krr).astype(BF16)
        kf_s[2, dst, :] = jnp.where(head_lo, krr, zero).astype(BF16)
        kf_s[3, dst, :] = jnp.where(head_lo, zero, kr).astype(BF16)
        vf_s[0, dst, :] = jnp.where(head_lo, vr, zero).astype(BF16)
        vf_s[1, dst, :] = jnp.where(head_lo, zero, vrr).astype(BF16)
        vf_s[2, dst, :] = jnp.where(head_lo, vrr, zero).astype(BF16)
        vf_s[3, dst, :] = jnp.where(head_lo, zero, vr).astype(BF16)
    _row_loop(T, R, p_rope)

    qi = lax.broadcasted_iota(jnp.int32, (W, 2 * W), 0)
    kj = lax.broadcasted_iota(jnp.int32, (W, 2 * W), 1)
    band = (kj >= qi) & (kj <= qi + W)
    for i in range(T // W):
        rows = slice(i * W, (i + 1) * W)
        krows = slice(i * W, i * W + 2 * W)
        if i == 0:
            valid = band & ((kj >= W) | (t > 0))
        else:
            valid = band
        bias = jnp.where(valid, 0.0, -jnp.inf)
        bias2 = jnp.concatenate([bias, bias], axis=1)
        for g in range(N_KV_C):
            kk2 = jnp.concatenate([kf_s[2 * g, krows, :], kf_s[2 * g + 1, krows, :]], axis=0)
            vv2 = jnp.concatenate([vf_s[2 * g, krows, :], vf_s[2 * g + 1, krows, :]], axis=0)
            for pp in range(G_C // 2):
                pair = g * (G_C // 2) + pp
                ls = slice(pair * LANES, (pair + 1) * LANES)
                s = _dot_nt(qr_s[rows, ls], kk2) + bias2
                probs = []
                for e in range(2):
                    head = 2 * pair + e
                    sh = s[:, e * 2 * W:(e + 1) * 2 * W]
                    sink = sinks_ref[0:1, head:head + 1]
                    m = jnp.maximum(jnp.max(sh, axis=1, keepdims=True), sink)
                    pe = jnp.exp(sh - m)
                    den = jnp.sum(pe, axis=1, keepdims=True) + jnp.exp(sink - m)
                    probs.append((pe / den).astype(BF16))
                o_s[rows, ls] = _dot(jnp.concatenate(probs, axis=1), vv2)

    def p_gate(r0):
        sl = pl.ds(r0, R)
        g_s[sl, :] = (o_s[sl, :] * _silu(pz_s[sl, :])).astype(BF16)
    _row_loop(T, R, p_gate)

    pq_s[...] = _dot(g_s[...], wo_ref[...])

    def p_out(r0):
        sl = pl.ds(r0, R)
        y_ref[0, sl, :] = x_ref[0, sl, :] + _rms(pq_s[sl, :], gpost_ref[...])
    _row_loop(T, R, p_out)

    @pl.when(t == NT - 1)
    def _():
        kw_ref[0] = kr_s[T - W:T, :]
        vw_ref[0] = pkv_s[T - W:T, LANES:2 * LANES]

    kf_s[:, 0:W, :] = kf_s[:, T:T + W, :]
    vf_s[:, 0:W, :] = vf_s[:, T:T + W, :]


def _odd_prompt(x, ow, tabs, T):
    B, L, D = x.shape
    NT = L // T
    consts = (ow['gpre'], ow['gpost'], ow['wq'], ow['wkv'], ow['wz'], ow['wo'], ow['sinks'])
    tab_spec = pl.BlockSpec((T, LANES), lambda b, t: (t, 0))
    in_specs = ([pl.BlockSpec((1, T, D), lambda b, t: (b, t, 0))] + [_const_spec(c.shape) for c in consts]
                + [tab_spec] * 4)
    out_shape = (
        jax.ShapeDtypeStruct((B, L, D), F32),
        jax.ShapeDtypeStruct((B, WINDOW, LANES), F32),
        jax.ShapeDtypeStruct((B, WINDOW, LANES), F32),
    )
    out_specs = (
        pl.BlockSpec((1, T, D), lambda b, t: (b, t, 0)),
        pl.BlockSpec((1, WINDOW, LANES), lambda b, t: (b, 0, 0)),
        pl.BlockSpec((1, WINDOW, LANES), lambda b, t: (b, 0, 0)),
    )
    scratch = [
        pltpu.VMEM((T, D), BF16),
        pltpu.VMEM((T, D), F32),
        pltpu.VMEM((T, D), F32),
        pltpu.VMEM((T, 2 * LANES), F32),
        pltpu.VMEM((T, D), BF16),
        pltpu.VMEM((T, LANES), F32),
        pltpu.VMEM((4, T + WINDOW, LANES), BF16),
        pltpu.VMEM((4, T + WINDOW, LANES), BF16),
        pltpu.VMEM((T, D), F32),
        pltpu.VMEM((T, D), BF16),
    ]
    return pl.pallas_call(
        functools.partial(_odd_prompt_body, T, NT),
        grid=(B, NT),
        in_specs=in_specs,
        out_specs=out_specs,
        out_shape=out_shape,
        scratch_shapes=scratch,
        compiler_params=pltpu.CompilerParams(
            dimension_semantics=("arbitrary", "arbitrary"), vmem_limit_bytes=VMEM_LIMIT),
        name="odd_prompt",
    )(x, *consts, *tabs)


def _sample_front_body(x_ref, cbs_ref, n_ref, mc_ref, mr_ref, gpre_ref, win_ref, wg_ref, wgt_ref, gbr_ref,
                       gbc_ref, cwb_ref, cbb_ref, wq_ref, wk_ref, wkt_ref,
                       anew_ref, az_ref, v_ref, op_ref, zb_ref, cbo_ref, uc_ref, q_ref, ktw_ref,
                       g0_ref, wv_ref, dn_ref, nn_ref, mn_ref):
    D = D_MODEL
    hn = _rms(x_ref[...], gpre_ref[...]).astype(BF16)

    def proj(p):
        return _dot(hn, win_ref[:, p * D:(p + 1) * D])
    anew_ref[...] = proj(0) * _sigmoid(proj(1))
    az_ref[...] = proj(2)
    u = proj(3)
    v_ref[...] = proj(4)
    op_ref[...] = proj(5)
    zb_ref[...] = proj(6)
    gc = _dot(hn, wg_ref[...]) + gbr_ref[...]
    gr = _dot_nt(wgt_ref[...], hn) + gbc_ref[...]

    acc = u * cwb_ref[CONV_B - 1:CONV_B, :]
    for j in range(CONV_B - 1):
        acc = acc + cbs_ref[:, j * D:(j + 1) * D] * cwb_ref[j:j + 1, :]
    uc = _silu(acc + cbb_ref[...])
    uc_ref[...] = uc
    cbo_ref[:, 0:(CONV_B - 2) * D] = cbs_ref[:, D:(CONV_B - 1) * D]
    cbo_ref[:, (CONV_B - 2) * D:] = u

    ucb = uc.astype(BF16)
    g0_ref[...] = jnp.zeros(g0_ref.shape, F32)
    wv_ref[...] = jnp.zeros(wv_ref.shape, F32)
    dn_ref[...] = jnp.ones(dn_ref.shape, F32)
    mn_ref[...] = jnp.zeros(mn_ref.shape, F32)
    for h in range(H_B):
        hs = slice(h * DH_B, (h + 1) * DH_B)
        ub = ucb[:, hs]
        q = _dot(ub, wq_ref[h])
        k = _dot(ub, wk_ref[h]) * K_SCALE_B
        kt = _dot_nt(wkt_ref[h], ub) * K_SCALE_B
        q_ref[:, hs] = q
        nrow = n_ref[:, hs]
        li_c = gc[:, h:h + 1]
        lf_c = _log_sigmoid(gc[:, H_B + h:H_B + h + 1])
        m_c = mc_ref[:, h:h + 1]
        m_new = jnp.maximum(lf_c + m_c, li_c)
        g0 = jnp.exp(lf_c + m_c - m_new)
        ws = jnp.exp(li_c - m_new)
        w = jnp.sum(q * k, axis=1, keepdims=True) * ws
        den = w + g0 * jnp.sum(q * nrow, axis=1, keepdims=True)
        g0_ref[:, h:h + 1] = g0
        wv_ref[:, h:h + 1] = w
        dn_ref[:, h:h + 1] = jnp.maximum(jnp.abs(den), jnp.exp(-m_new))
        mn_ref[:, h:h + 1] = m_new
        nn_ref[:, hs] = g0 * nrow + ws * k
        li_r = gr[h:h + 1, :]
        lf_r = _log_sigmoid(gr[H_B + h:H_B + h + 1, :])
        m_r = mr_ref[h:h + 1, :]
        ws_r = jnp.exp(li_r - jnp.maximum(lf_r + m_r, li_r))
        ktw_ref[h] = (kt * ws_r).astype(BF16)


def _sample_state_body(N, ca_ref, anew_ref, cwa_ref, c_ref, q_ref, v_ref, ktw_ref, g0_ref,
                       cao_ref, co_ref, cn_ref, numi_ref):
    D = D_MODEL
    bb = SAMPLE_BLOCK
    i = pl.program_id(0)
    nst = CONV_A - 1
    a_new = anew_ref[...]
    acc = a_new * cwa_ref[nst:nst + 1, :]
    for r in range(nst):
        acc = acc + ca_ref[:, r * D:(r + 1) * D] * cwa_ref[r:r + 1, :]
    co_ref[...] = acc
    cao_ref[:, 0:(nst - 1) * D] = ca_ref[:, D:nst * D]
    cao_ref[:, (nst - 1) * D:] = a_new

    rown = lax.broadcasted_iota(jnp.int32, (N, DH_B), 0)
    rowb = lax.broadcasted_iota(jnp.int32, (bb, DH_B), 0)
    for h in range(H_B):
        hs = slice(h * DH_B, (h + 1) * DH_B)
        qh = q_ref[:, hs].astype(BF16)
        vh = v_ref[:, hs]
        ktw = ktw_ref[h]
        numi = jnp.zeros((bb, DH_B), F32)
        for j in range(bb):
            cm = c_ref[j, h]
            r = _dot(qh, cm.astype(BF16))
            numi = jnp.where(rowb == j, r, numi)
            vsel = jnp.where(rown == i * bb + j, vh, 0.0).astype(BF16)
            cn_ref[j, h] = g0_ref[j:j + 1, h:h + 1] * cm + _dot(ktw, vsel)
        numi_ref[:, hs] = numi


def _sample_mid_body(NB, x_ref, co_ref, az_ref, numi_ref, v_ref, wv_ref, g0_ref, dn_ref, op_ref, zb_ref, uc_ref,
                     cba_ref, lng_ref, lnb_ref, hng_ref, skip_ref, wout_ref, gpost_ref,
                     gpre_ref, wqs_ref, wkv_ref, wzs_ref, cq_ref, sq_ref, ck_ref, sk_ref,
                     x1_ref, qblk_ref, kn_ref, vn_ref, sz_ref, ycat_s):
    D = D_MODEL
    bb = SAMPLE_BLOCK
    ya = _silu(_layernorm(co_ref[...] + cba_ref[...], lng_ref[...], lnb_ref[...])) * _silu(az_ref[...])
    ycat_s[:, 0:D] = ya.astype(BF16)
    for h in range(H_B):
        hs = slice(h * DH_B, (h + 1) * DH_B)
        num = wv_ref[:, h:h + 1] * v_ref[:, hs] + g0_ref[:, h:h + 1] * numi_ref[:, hs]
        o = _sigmoid(op_ref[:, hs]) * (num / dn_ref[:, h:h + 1])
        hb = _rms(o, hng_ref[:, hs])
        hb = (hb + skip_ref[:, hs] * uc_ref[:, hs]) * _silu(zb_ref[:, hs])
        ycat_s[:, D + h * DH_B:D + (h + 1) * DH_B] = hb.astype(BF16)
    x1 = x_ref[...] + _rms(_dot(ycat_s[...], wout_ref[...]), gpost_ref[...])
    x1_ref[...] = x1

    hn = _rms(x1, gpre_ref[...]).astype(BF16)
    lane = lax.broadcasted_iota(jnp.int32, (x1.shape[0], LANES), 1)
    lane_lo = (lane % HD_C) < (ROT_DIM // 2)
    kv = _dot(hn, wkv_ref[...])
    kn_ref[...] = _rope_slab(kv[:, 0:LANES], ck_ref[...], sk_ref[...], lane_lo)
    vn_ref[...] = kv[:, LANES:2 * LANES]
    sz_ref[...] = _silu(_dot(hn, wzs_ref[...]))
    qs = _dot(hn, wqs_ref[...])
    for h in range(H_C):
        qh = _rope_slab(qs[:, h * LANES:(h + 1) * LANES], cq_ref[...], sq_ref[...], lane_lo)
        for blk in range(NB):
            qblk_ref[blk, h * bb:(h + 1) * bb, :] = qh[blk * bb:(blk + 1) * bb, :]


def _sample_attn_body(q_ref, kc_ref, vc_ref, kn_ref, vn_ref, sink_ref, o_ref, kco_ref, vco_ref):
    bb = SAMPLE_BLOCK
    W = WINDOW
    q = q_ref[0]
    qb = q.astype(BF16)
    rowj = lax.broadcasted_iota(jnp.int32, q.shape, 0) % bb
    sink = sink_ref[:, 0:1]
    acc = jnp.zeros(q.shape, F32)
    for j in range(bb):
        kj = kc_ref[j]
        vj = vc_ref[j]
        kn = kn_ref[j:j + 1, :]
        vn = vn_ref[j:j + 1, :]
        s = _dot_nt(qb, kj.astype(BF16))
        sn = jnp.sum(q * kn, axis=1, keepdims=True)
        m = jnp.maximum(jnp.maximum(jnp.max(s, axis=1, keepdims=True), sn), sink)
        p = jnp.exp(s - m)
        pn = jnp.exp(sn - m)
        den = jnp.sum(p, axis=1, keepdims=True) + pn + jnp.exp(sink - m)
        o = (_dot(p.astype(BF16), vj.astype(BF16)) + pn * vn) / den
        acc = jnp.where(rowj == j, o, acc)
        kco_ref[j, 0:W - 1, :] = kc_ref[j, 1:W, :]
        kco_ref[j, W - 1:W, :] = kn
        vco_ref[j, 0:W - 1, :] = vc_ref[j, 1:W, :]
        vco_ref[j, W - 1:W, :] = vn
    o_ref[0] = acc


def _sample_back_body(NB, o_ref, sz_ref, x1_ref, wos_ref, gpost_ref, y_ref, g_s):
    bb = SAMPLE_BLOCK
    for blk in range(NB):
        rs = slice(blk * bb, (blk + 1) * bb)
        for h in range(H_C):
            ls = slice(h * LANES, (h + 1) * LANES)
            g_s[rs, ls] = o_ref[blk, h * bb:(h + 1) * bb, :] * sz_ref[rs, ls]
    y = _dot(g_s[...].astype(BF16), wos_ref[...])
    y_ref[...] = x1_ref[...] + _rms(y, gpost_ref[...])


def _sample_path(x_s, st_ca, st_cb, st_c, st_n, st_m, ck_win, cv_win, ew, ow, osw):
    N, D = x_s.shape
    bb = SAMPLE_BLOCK
    NB = N // bb
    nst = CONV_A - 1
    f = lambda *shape: jax.ShapeDtypeStruct(shape, F32)
    cparams = pltpu.CompilerParams(vmem_limit_bytes=VMEM_LIMIT)

    m_col = jnp.pad(st_m, ((0, 0), (0, LANES - H_B)))
    m_row = jnp.pad(st_m.T, ((0, 8 - H_B), (0, 0)))
    front_out = (f(N, D), f(N, D), f(N, D), f(N, D), f(N, D), f(N, (CONV_B - 1) * D), f(N, D), f(N, D),
                 jax.ShapeDtypeStruct((H_B, DH_B, N), BF16), f(N, LANES), f(N, LANES), f(N, LANES), f(N, D),
                 f(N, LANES))
    (a_new, az, v, opre, zb, cb_new, uc, q, ktw, g0, wv, dn, n_new, m_new) = pl.pallas_call(
        _sample_front_body, out_shape=front_out, compiler_params=cparams, name="sample_front",
    )(x_s, st_cb.reshape(N, (CONV_B - 1) * D), st_n.reshape(N, D), m_col, m_row,
      ew['gpre'], ew['win'], ew['wg'], ew['wgt'], ew['gbr'], ew['gbc'], ew['cwb'], ew['cbb'],
      ew['wq'], ew['wk'], ew['wkt'])

    blk2 = lambda w: pl.BlockSpec((bb, w), lambda i: (i, 0))
    ca_new, conv_out, c_new, numi = pl.pallas_call(
        functools.partial(_sample_state_body, N),
        grid=(NB,),
        in_specs=[blk2(nst * D), blk2(D), _const_spec(ew['cwa'].shape),
                  pl.BlockSpec((bb, H_B, DH_B, DH_B), lambda i: (i, 0, 0, 0)),
                  blk2(D), _const_spec((N, D)), _const_spec((H_B, DH_B, N)), blk2(LANES)],
        out_specs=(blk2(nst * D), blk2(D), pl.BlockSpec((bb, H_B, DH_B, DH_B), lambda i: (i, 0, 0, 0)), blk2(D)),
        out_shape=(f(N, nst * D), f(N, D), f(N, H_B, DH_B, DH_B), f(N, D)),
        compiler_params=pltpu.CompilerParams(dimension_semantics=("arbitrary",), vmem_limit_bytes=VMEM_LIMIT),
        name="sample_state",
    )(st_ca.reshape(N, nst * D), a_new, ew['cwa'], st_c, q, v, ktw, g0)

    tabs = _rope_tables(jnp.full((1,), PAST_LEN, dtype=jnp.int32))
    x1, qblk, k_new, v_new, sz = pl.pallas_call(
        functools.partial(_sample_mid_body, NB),
        out_shape=(f(N, D), f(NB, H_C * bb, LANES), f(N, LANES), f(N, LANES), f(N, H_C * LANES)),
        scratch_shapes=[pltpu.VMEM((N, 2 * D), BF16)],
        compiler_params=cparams, name="sample_mid",
    )(x_s, conv_out, az, numi, v, wv, g0, dn, opre, zb, uc,
      ew['cba'], ew['lng'], ew['lnb'], ew['hng'], ew['skip'], ew['wout'], ew['gpost'],
      ow['gpre'], osw['wqs'], ow['wkv'], osw['wzs'], *tabs)

    blk3 = pl.BlockSpec((bb, WINDOW, LANES), lambda i: (i, 0, 0))
    qspec = pl.BlockSpec((1, H_C * bb, LANES), lambda i: (i, 0, 0))
    oblk, kc_new, vc_new = pl.pallas_call(
        _sample_attn_body,
        grid=(NB,),
        in_specs=[qspec, blk3, blk3, blk2(LANES), blk2(LANES), _const_spec((H_C * bb, LANES))],
        out_specs=(qspec, blk3, blk3),
        out_shape=(f(NB, H_C * bb, LANES), f(N, WINDOW, LANES), f(N, WINDOW, LANES)),
        compiler_params=pltpu.CompilerParams(dimension_semantics=("arbitrary",), vmem_limit_bytes=VMEM_LIMIT),
        name="sample_attn",
    )(qblk, ck_win.reshape(N, WINDOW, LANES), cv_win.reshape(N, WINDOW, LANES), k_new, v_new, osw['sinkcol'])

    y_s = pl.pallas_call(
        functools.partial(_sample_back_body, NB),
        out_shape=f(N, D),
        scratch_shapes=[pltpu.VMEM((N, H_C * LANES), F32)],
        compiler_params=cparams, name="sample_back",
    )(oblk, sz, x1, osw['wos'], ow['gpost'])

    return (y_s, ca_new.reshape(N, nst, D), cb_new.reshape(N, CONV_B - 1, D), c_new,
            n_new.reshape(N, H_B, DH_B), m_new[:, :H_B], kc_new, vc_new)


def _prep_even(j, norm_pre_e, norm_post_e, w_in_e, conv_a_w, conv_a_b, ln_a_g, ln_a_b, conv_b_w, conv_b_b,
               wq_b, wk_b, b_i, b_f, headnorm_b, skip_b, w_out_e):
    D = D_MODEL
    w_in = w_in_e[j]
    wgates = w_in[:, 7 * D:]
    gbias = jnp.concatenate([b_i[j], b_f[j]])
    row2 = lambda a: a.reshape(1, -1)
    return dict(
        gpre=row2(norm_pre_e[j]), gpost=row2(norm_post_e[j]),
        win=w_in[:, :7 * D].astype(BF16),
        wg=jnp.pad(wgates, ((0, 0), (0, LANES - 2 * H_B))).astype(BF16),
        wgt=wgates.T.astype(BF16),
        gbr=jnp.pad(gbias, (0, LANES - 2 * H_B)).reshape(1, LANES),
        gbc=gbias.reshape(2 * H_B, 1),
        cwa=conv_a_w[j], cba=row2(conv_a_b[j]), lng=row2(ln_a_g[j]), lnb=row2(ln_a_b[j]),
        cwb=conv_b_w[j], cbb=row2(conv_b_b[j]),
        wq=wq_b[j].astype(BF16), wk=wk_b[j].astype(BF16),
        wkt=jnp.swapaxes(wk_b[j], 1, 2).astype(BF16),
        hng=row2(headnorm_b[j]), skip=row2(skip_b[j]),
        wout=w_out_e[j].astype(BF16),
    )


def _prep_odd(j, norm_pre_o, norm_post_o, w_in_o, sinks, w_out_o):
    D = D_MODEL
    w_in = w_in_o[j]
    row2 = lambda a: a.reshape(1, -1)
    return dict(
        gpre=row2(norm_pre_o[j]), gpost=row2(norm_post_o[j]),
        wq=w_in[:, :D].astype(BF16),
        wkv=w_in[:, D:D + 2 * LANES].astype(BF16),
        wz=w_in[:, D + 2 * LANES:].astype(BF16),
        wo=w_out_o[j].astype(BF16),
        sinks=jnp.pad(sinks[j], (0, LANES - H_C)).reshape(1, LANES),
    )


def _prep_odd_sample(j, w_in_o, sinks, w_out_o):
    D = D_MODEL
    w_in = w_in_o[j]
    on_group = (jnp.arange(H_C)[:, None] // G_C) == jnp.arange(N_KV_C)[None, :]

    def stack_cols(w):
        w4 = w.reshape(D, H_C, 1, HD_C)
        return jnp.where(on_group[None, :, :, None], w4, 0.0).reshape(D, H_C * LANES).astype(BF16)
    wo4 = w_out_o[j].reshape(H_C, 1, HD_C, D)
    return dict(
        wqs=stack_cols(w_in[:, :D]),
        wzs=stack_cols(w_in[:, D + 2 * LANES:]),
        wos=jnp.where(on_group[:, :, None, None], wo4, 0.0).reshape(H_C * LANES, D).astype(BF16),
        sinkcol=jnp.broadcast_to(jnp.repeat(sinks[j], SAMPLE_BLOCK)[:, None], (H_C * SAMPLE_BLOCK, LANES)),
    )


def _rope_tables(pos):
    half = ROT_DIM // 2
    inv = jnp.power(ROPE_THETA, -jnp.arange(half, dtype=F32) * (2.0 / ROT_DIM))
    ang = pos.astype(F32)[:, None] * inv[None, :]
    cos, sin = jnp.cos(ang), jnp.sin(ang)
    n = pos.shape[0]
    pad = HD_C - ROT_DIM
    cos64 = jnp.concatenate([cos, cos, jnp.ones((n, pad), F32)], axis=1)
    sin64 = jnp.concatenate([-sin, sin, jnp.zeros((n, pad), F32)], axis=1)
    ck = jnp.concatenate([cos64, cos64], axis=1)
    sk = jnp.concatenate([sin64, sin64], axis=1)
    return ck * Q_SCALE_C, sk * Q_SCALE_C, ck, sk


def kernel(x_prompt, x_sample, state_conv_a, state_conv_b, state_mlstm_c, state_mlstm_n, state_mlstm_m, cache_k_win, cache_v_win, norm_pre_e, norm_post_e, w_in_e, conv_a_w, conv_a_b, ln_a_g, ln_a_b, conv_b_w, conv_b_b, wq_b, wk_b, b_i, b_f, headnorm_b, skip_b, w_out_e, norm_pre_o, norm_post_o, w_in_o, sinks, w_out_o):
    B, L, D = x_prompt.shape
    ew = _prep_even(0, norm_pre_e, norm_post_e, w_in_e, conv_a_w, conv_a_b, ln_a_g, ln_a_b, conv_b_w,
                    conv_b_b, wq_b, wk_b, b_i, b_f, headnorm_b, skip_b, w_out_e)
    ow = _prep_odd(0, norm_pre_o, norm_post_o, w_in_o, sinks, w_out_o)

    x1, ca_p, cb_p, c_p, n_p, m_p = _even_prompt(x_prompt, ew, min(TILE_EVEN, L))
    tabs_p = _rope_tables(jnp.arange(L, dtype=jnp.int32))
    y_p, kw_p, vw_p = _odd_prompt(x1, ow, tabs_p, min(TILE_ODD, L))

    conv_a_p = ca_p[None, :, 32 - (CONV_A - 1):, :]
    conv_b_p = cb_p[None, :, 8 - (CONV_B - 1):, :]
    mlstm_c_p = c_p[None]
    mlstm_n_p = n_p[None]
    mlstm_m_p = m_p[None, :, :H_B, 0]
    k_win_p = kw_p.reshape(1, B, WINDOW, N_KV_C, HD_C)
    v_win_p = vw_p.reshape(1, B, WINDOW, N_KV_C, HD_C)

    N = x_sample.shape[0]
    osw = _prep_odd_sample(0, w_in_o, sinks, w_out_o)
    y_s, ca_s, cb_s, c_s, n_s, m_s, kw_s, vw_s = _sample_path(
        x_sample.reshape(N, D), state_conv_a[0], state_conv_b[0], state_mlstm_c[0], state_mlstm_n[0],
        state_mlstm_m[0], cache_k_win[0], cache_v_win[0], ew, ow, osw)
    return (y_p, y_s.reshape(N, 1, D), conv_a_p, conv_b_p, mlstm_c_p, mlstm_n_p, mlstm_m_p, k_win_p, v_win_p,
            ca_s[None], cb_s[None], c_s[None], n_s[None], m_s[None],
            kw_s.reshape(1, N, WINDOW, N_KV_C, HD_C), vw_s.reshape(1, N, WINDOW, N_KV_C, HD_C))
```

```python
import functools

import jax
import jax.numpy as jnp
from jax import lax
from jax.experimental import pallas as pl
from jax.experimental.pallas import tpu as pltpu

F32 = jnp.float32
BF16 = jnp.bfloat16

D_MODEL = 1024
EPS = 1e-6
PAST_LEN = 16384
CONV_A = 31
CONV_B = 4
H_B = 4
DH_B = D_MODEL // H_B
HD_C = 64
H_C = D_MODEL // HD_C
N_KV_C = 2
G_C = H_C // N_KV_C
WINDOW = 128
ROT_DIM = HD_C // 4
ROPE_THETA = 500000.0
K_SCALE_B = DH_B ** -0.5
Q_SCALE_C = HD_C ** -0.5

LANES = 128
NG = D_MODEL // LANES
STRIDE = 4
TILE_EVEN = 256
TILE_ODD = 256
ROWS = 16
SAMPLE_BLOCK = 8
VMEM_LIMIT = 56 * 1024 * 1024

NT_DIMS = (((1,), (1,)), ((), ()))
TN_DIMS = (((0,), (0,)), ((), ()))


def _dot(a, b):
    return jnp.dot(a, b, preferred_element_type=F32)


def _dot_nt(a, b):
    return lax.dot_general(a, b, NT_DIMS, preferred_element_type=F32)


def _sigmoid(x):
    return 1.0 / (1.0 + jnp.exp(-x))


def _silu(x):
    return x * _sigmoid(x)


def _log_sigmoid(x):
    return jnp.minimum(x, 0.0) - jnp.log1p(jnp.exp(-jnp.abs(x)))


def _rms(x, g):
    ms = jnp.mean(x * x, axis=-1, keepdims=True)
    return x * lax.rsqrt(ms + EPS) * g


def _layernorm(x, g, b):
    mu = jnp.mean(x, axis=-1, keepdims=True)
    xc = x - mu
    var = jnp.mean(xc * xc, axis=-1, keepdims=True)
    return xc * lax.rsqrt(var + EPS) * g + b


def _row_loop(total, rows, body):
    def step(i, carry):
        body(pl.multiple_of(i * rows, rows))
        return carry
    lax.fori_loop(0, total // rows, step, 0)


def _split3(a):
    hi = a.astype(BF16)
    r1 = a - hi.astype(F32)
    mid = r1.astype(BF16)
    lo = (r1 - mid.astype(F32)).astype(BF16)
    return hi, mid, lo


def _conv_strided(src, w_ref, dst, taps, off, total):
    chunk_rows = 8 * STRIDE

    def chunk(c, carry):
        base = pl.multiple_of(c * chunk_rows, chunk_rows)
        for g in range(NG):
            gs = slice(g * LANES, (g + 1) * LANES)
            accs = [None] * STRIDE
            for j in range(taps):
                wj = jnp.broadcast_to(w_ref[j:j + 1, gs], (8, LANES))
                for p in range(STRIDE):
                    v = src[g, pl.ds(base + (p + off + j), 8, stride=STRIDE), :] * wj
                    accs[p] = v if j == 0 else accs[p] + v
            for p in range(STRIDE):
                dst[g, pl.ds(base + p, 8, stride=STRIDE), :] = accs[p]
        return carry
    lax.fori_loop(0, total // chunk_rows, chunk, 0)


def _even_prompt_body(T, NT, x_ref, gpre_ref, gpost_ref, win_ref, wg_ref, wgt_ref, gbr_ref, gbc_ref,
                      cwa_ref, cba_ref, lng_ref, lnb_ref, cwb_ref, cbb_ref, wq_ref, wk_ref, wkt_ref,
                      hng_ref, skip_ref, wout_ref,
                      y_ref, ca_ref, cb_ref, c_ref, n_ref, m_ref,
                      hn_s, p_s, abuf, ubuf, cbuf, uc_s, ucb_s, ycat_s, cst_s, nst_s, mst_s,
                      st_s, st2_s, rinv_s, st4_s, rinv4_s):
    D = D_MODEL
    R = ROWS
    t = pl.program_id(1)

    @pl.when(t == 0)
    def _():
        abuf[:, 0:32, :] = jnp.zeros((NG, 32, LANES), F32)
        ubuf[:, 0:8, :] = jnp.zeros((NG, 8, LANES), F32)
        cst_s[...] = jnp.zeros(cst_s.shape, F32)
        nst_s[...] = jnp.zeros(nst_s.shape, F32)
        mst_s[...] = jnp.zeros(mst_s.shape, F32)

    _rms_scale(T, lambda sl, gs: x_ref[0, sl, gs], st_s, rinv_s, D)

    def p_norm(r0):
        sl = pl.ds(r0, R)
        rinv = rinv_s[sl, :]
        for g in range(NG):
            gs = slice(g * LANES, (g + 1) * LANES)
            hn_s[sl, gs] = (x_ref[0, sl, gs] * rinv * gpre_ref[:, gs]).astype(BF16)
    _row_loop(T, R, p_norm)

    hn = hn_s[...]

    def proj(p):
        return _dot(hn, win_ref[:, p * D:(p + 1) * D])
    p_s[0] = proj(0)
    p_s[1] = proj(1)
    p_s[2] = proj(2)
    u_new = proj(3)
    for g in range(NG):
        ubuf[g, 8:8 + T, :] = u_new[:, g * LANES:(g + 1) * LANES]
    p_s[3] = proj(4)
    p_s[4] = proj(5)
    p_s[5] = proj(6)
    gc = _dot(hn, wg_ref[...]) + gbr_ref[...]
    gr = _dot_nt(wgt_ref[...], hn) + gbc_ref[...]

    row = lax.broadcasted_iota(jnp.int32, (T, T), 0)
    col = lax.broadcasted_iota(jnp.int32, (T, T), 1)
    causal = row >= col
    tril = jnp.where(causal, 1.0, 0.0).astype(BF16)
    triu = jnp.where(row <= col, 1.0, 0.0).astype(BF16)
    lf_c = _log_sigmoid(gc)
    lf_r = _log_sigmoid(gr)
    ch, cm, cl = _split3(lf_c)
    b_c = _dot(tril, ch) + _dot(tril, cm) + _dot(tril, cl)
    rh, rm, rl = _split3(lf_r)
    b_r = _dot(rh, triu) + _dot(rm, triu) + _dot(rl, triu)

    def p_glu(r0):
        sl = pl.ds(r0, R)
        for g in range(NG):
            gs = slice(g * LANES, (g + 1) * LANES)
            abuf[g, pl.ds(r0 + 32, R), :] = p_s[0, sl, gs] * _sigmoid(p_s[1, sl, gs])
    _row_loop(T, R, p_glu)

    _conv_strided(abuf, cwa_ref, cbuf, CONV_A, 32 - (CONV_A - 1), T)

    ones = jnp.ones((LANES, LANES), BF16)

    def p_ln_mean(r0):
        sl = pl.ds(r0, R)
        acc = None
        for g in range(NG):
            x = cbuf[g, sl, :] + cba_ref[:, g * LANES:(g + 1) * LANES]
            cbuf[g, sl, :] = x
            acc = x if acc is None else acc + x
        hi = acc.astype(BF16)
        st_s[sl, :] = hi
        st2_s[sl, :] = (acc - hi.astype(F32)).astype(BF16)
    _row_loop(T, R, p_ln_mean)
    rinv_s[...] = (_dot(st_s[...], ones) + _dot(st2_s[...], ones)) * (1.0 / D)

    def p_ln_center(r0):
        sl = pl.ds(r0, R)
        mu = rinv_s[sl, :]
        acc = None
        for g in range(NG):
            xc = cbuf[g, sl, :] - mu
            cbuf[g, sl, :] = xc
            acc = xc * xc if acc is None else acc + xc * xc
        st_s[sl, :] = acc.astype(BF16)
    _row_loop(T, R, p_ln_center)
    rinv_s[...] = lax.rsqrt(_dot(st_s[...], ones) * (1.0 / D) + EPS)

    def p_ln(r0):
        sl = pl.ds(r0, R)
        rs = rinv_s[sl, :]
        for g in range(NG):
            gs = slice(g * LANES, (g + 1) * LANES)
            yv = cbuf[g, sl, :] * rs * lng_ref[:, gs] + lnb_ref[:, gs]
            ycat_s[sl, gs] = (_silu(yv) * _silu(p_s[2, sl, gs])).astype(BF16)
    _row_loop(T, R, p_ln)

    _conv_strided(ubuf, cwb_ref, cbuf, CONV_B, 8 - (CONV_B - 1), T)

    def p_uc(r0):
        sl = pl.ds(r0, R)
        for g in range(NG):
            gs = slice(g * LANES, (g + 1) * LANES)
            uc = _silu(cbuf[g, sl, :] + cbb_ref[:, gs])
            uc_s[sl, gs] = uc
            ucb_s[sl, gs] = uc.astype(BF16)
    _row_loop(T, R, p_uc)

    for h in range(H_B):
        hs = slice(h * DH_B, (h + 1) * DH_B)
        ub = ucb_s[:, hs]
        q = _dot(ub, wq_ref[h])
        k = _dot(ub, wk_ref[h]) * K_SCALE_B
        kt = _dot_nt(wkt_ref[h], ub) * K_SCALE_B
        qb = q.astype(BF16)
        s = _dot(qb, kt.astype(BF16))
        bc = b_c[:, 4 + h:5 + h]
        br = b_r[4 + h:5 + h, :]
        lic = gc[:, h:h + 1]
        lir = gr[h:h + 1, :]
        m_h = mst_s[h:h + 1, 0:1]
        dm = jnp.where(causal, bc - br + lir, -jnp.inf)
        inter = bc + m_h
        mt = jnp.maximum(inter, jnp.max(dm, axis=1, keepdims=True))
        w = s * jnp.exp(dm - mt)
        a_in = jnp.exp(inter - mt)
        vb = p_s[3, :, hs].astype(BF16)
        cmat = cst_s[h]
        nrow = nst_s[h:h + 1, :]
        num = _dot(w.astype(BF16), vb) + a_in * _dot(qb, cmat.astype(BF16))
        den = jnp.sum(w, axis=1, keepdims=True) + a_in * jnp.sum(q * nrow, axis=1, keepdims=True)
        p_s[0, :, hs] = num / jnp.maximum(jnp.abs(den), jnp.exp(-mt))
        bl = bc[T - 1:T, :]
        wl_r = bl - br + lir
        wl_c = bl - bc + lic
        m_new = jnp.maximum(bl + m_h, jnp.max(wl_r, axis=1, keepdims=True))
        g0 = jnp.exp(bl + m_h - m_new)
        ws_r = jnp.exp(wl_r - m_new)
        ws_c = jnp.exp(wl_c - m_new)
        cst_s[h] = g0 * cmat + _dot((kt * ws_r).astype(BF16), vb)
        nst_s[h:h + 1, :] = g0 * nrow + jnp.sum(ws_c * k, axis=0, keepdims=True)
        mst_s[h:h + 1, :] = jnp.broadcast_to(m_new, (1, LANES))

    gph = DH_B // LANES

    def p_ogate(r0):
        sl = pl.ds(r0, R)
        for h in range(H_B):
            acc = None
            for gg in range(gph):
                gs = slice((h * gph + gg) * LANES, (h * gph + gg + 1) * LANES)
                o = _sigmoid(p_s[4, sl, gs]) * p_s[0, sl, gs]
                p_s[0, sl, gs] = o
                acc = o * o if acc is None else acc + o * o
            st4_s[sl, h * LANES:(h + 1) * LANES] = acc.astype(BF16)
    _row_loop(T, R, p_ogate)
    for h in range(H_B):
        hl = slice(h * LANES, (h + 1) * LANES)
        rinv4_s[:, hl] = lax.rsqrt(_dot(st4_s[:, hl], ones) * (1.0 / DH_B) + EPS)

    def p_hb(r0):
        sl = pl.ds(r0, R)
        for g in range(NG):
            gs = slice(g * LANES, (g + 1) * LANES)
            h = g // gph
            hb = p_s[0, sl, gs] * rinv4_s[sl, h * LANES:(h + 1) * LANES] * hng_ref[:, gs]
            hb = (hb + skip_ref[:, gs] * uc_s[sl, gs]) * _silu(p_s[5, sl, gs])
            ycat_s[sl, D + g * LANES:D + (g + 1) * LANES] = hb.astype(BF16)
    _row_loop(T, R, p_hb)

    p_s[1] = _dot(ycat_s[...], wout_ref[...])
    _rms_scale(T, lambda sl, gs: p_s[1, sl, gs], st_s, rinv_s, D)

    def p_out(r0):
        sl = pl.ds(r0, R)
        rinv = rinv_s[sl, :]
        for g in range(NG):
            gs = slice(g * LANES, (g + 1) * LANES)
            y_ref[0, sl, gs] = x_ref[0, sl, gs] + p_s[1, sl, gs] * rinv * gpost_ref[:, gs]
    _row_loop(T, R, p_out)

    @pl.when(t == NT - 1)
    def _():
        for g in range(NG):
            ca_ref[0, :, g * LANES:(g + 1) * LANES] = abuf[g, T:T + 32, :]
            cb_ref[0, :, g * LANES:(g + 1) * LANES] = ubuf[g, T:T + 8, :]
        c_ref[0] = cst_s[...]
        n_ref[0] = nst_s[0:H_B, :]
        m_ref[0] = mst_s[...]

    abuf[:, 0:32, :] = abuf[:, T:T + 32, :]
    ubuf[:, 0:8, :] = ubuf[:, T:T + 8, :]


def _const_spec(shape):
    nd = len(shape)
    return pl.BlockSpec(shape, lambda *_: (0,) * nd, pipeline_mode=pl.Buffered(1))


def _even_prompt(x, ew, T):
    B, L, D = x.shape
    NT = L // T
    consts = (ew['gpre'], ew['gpost'], ew['win'], ew['wg'], ew['wgt'], ew['gbr'], ew['gbc'],
              ew['cwa'], ew['cba'], ew['lng'], ew['lnb'], ew['cwb'], ew['cbb'], ew['wq'], ew['wk'],
              ew['wkt'], ew['hng'], ew['skip'], ew['wout'])
    in_specs = [pl.BlockSpec((1, T, D), lambda b, t: (b, t, 0))] + [_const_spec(c.shape) for c in consts]
    out_shape = (
        jax.ShapeDtypeStruct((B, L, D), F32),
        jax.ShapeDtypeStruct((B, 32, D), F32),
        jax.ShapeDtypeStruct((B, 8, D), F32),
        jax.ShapeDtypeStruct((B, H_B, DH_B, DH_B), F32),
        jax.ShapeDtypeStruct((B, H_B, DH_B), F32),
        jax.ShapeDtypeStruct((B, 8, LANES), F32),
    )
    out_specs = (
        pl.BlockSpec((1, T, D), lambda b, t: (b, t, 0)),
        pl.BlockSpec((1, 32, D), lambda b, t: (b, 0, 0)),
        pl.BlockSpec((1, 8, D), lambda b, t: (b, 0, 0)),
        pl.BlockSpec((1, H_B, DH_B, DH_B), lambda b, t: (b, 0, 0, 0)),
        pl.BlockSpec((1, H_B, DH_B), lambda b, t: (b, 0, 0)),
        pl.BlockSpec((1, 8, LANES), lambda b, t: (b, 0, 0)),
    )
    scratch = [
        pltpu.VMEM((T, D), BF16),
        pltpu.VMEM((6, T, D), F32),
        pltpu.VMEM((NG, T + 32, LANES), F32),
        pltpu.VMEM((NG, T + 8, LANES), F32),
        pltpu.VMEM((NG, T, LANES), F32),
        pltpu.VMEM((T, D), F32),
        pltpu.VMEM((T, D), BF16),
        pltpu.VMEM((T, 2 * D), BF16),
        pltpu.VMEM((H_B, DH_B, DH_B), F32),
        pltpu.VMEM((8, DH_B), F32),
        pltpu.VMEM((8, LANES), F32),
        pltpu.VMEM((T, LANES), BF16),
        pltpu.VMEM((T, LANES), BF16),
        pltpu.VMEM((T, LANES), F32),
        pltpu.VMEM((T, H_B * LANES), BF16),
        pltpu.VMEM((T, H_B * LANES), F32),
    ]
    return pl.pallas_call(
        functools.partial(_even_prompt_body, T, NT),
        grid=(B, NT),
        in_specs=in_specs,
        out_specs=out_specs,
        out_shape=out_shape,
        scratch_shapes=scratch,
        compiler_params=pltpu.CompilerParams(
            dimension_semantics=("arbitrary", "arbitrary"), vmem_limit_bytes=VMEM_LIMIT),
        name="even_prompt",
    )(x, *consts)


def _rope_slab(x, cos, sin, lane_lo):
    partner = jnp.where(lane_lo, pltpu.roll(x, LANES - ROT_DIM // 2, 1), pltpu.roll(x, ROT_DIM // 2, 1))
    return x * cos + partner * sin


def _rms_scale(T, src, st_s, rinv_s, width):
    R = ROWS
    ng = width // LANES

    def p_sq(r0):
        sl = pl.ds(r0, R)
        acc = None
        for g in range(ng):
            xg = src(sl, slice(g * LANES, (g + 1) * LANES))
            acc = xg * xg if acc is None else acc + xg * xg
        st_s[sl, :] = acc.astype(BF16)
    _row_loop(T, R, p_sq)
    ms = _dot(st_s[...], jnp.ones((LANES, LANES), BF16))
    rinv_s[...] = lax.rsqrt(ms * (1.0 / width) + EPS)


def _odd_prompt_body(T, NT, x_ref, gpre_ref, gpost_ref, wqt_ref, wkv_ref, wz_ref, wo_ref, sinks_ref,
                     cqt_ref, sqt_ref, ck_ref, sk_ref,
                     y_ref, kw_ref, vw_ref,
                     hn_s, st_s, rinv_s, py_s, pz_s, pkv_s, qt_s, kr_s, kf_s, vf_s, o_s, g_s):
    D = D_MODEL
    R = ROWS
    W = WINDOW
    t = pl.program_id(1)

    @pl.when(t == 0)
    def _():
        kf_s[:, 0:W, :] = jnp.zeros((4, W, LANES), BF16)
        vf_s[:, 0:W, :] = jnp.zeros((4, W, LANES), BF16)

    _rms_scale(T, lambda sl, gs: x_ref[0, sl, gs], st_s, rinv_s, D)

    def p_norm(r0):
        sl = pl.ds(r0, R)
        rinv = rinv_s[sl, :]
        for g in range(NG):
            gs = slice(g * LANES, (g + 1) * LANES)
            hn_s[sl, gs] = (x_ref[0, sl, gs] * rinv * gpre_ref[:, gs]).astype(BF16)
    _row_loop(T, R, p_norm)

    hn = hn_s[...]
    qt = _dot_nt(wqt_ref[...], hn)
    pkv_s[...] = _dot(hn, wkv_ref[...])
    pz_s[...] = _dot(hn, wz_ref[...])

    cq, sq = cqt_ref[...], sqt_ref[...]
    half = ROT_DIM // 2
    for h in range(H_C):
        b0 = h * HD_C
        x1 = qt[b0:b0 + half, :]
        x2 = qt[b0 + half:b0 + ROT_DIM, :]
        rot = jnp.concatenate([x1 * cq - x2 * sq, x2 * cq + x1 * sq, qt[b0 + ROT_DIM:b0 + HD_C, :]], axis=0)
        qt_s[b0:b0 + HD_C, :] = rot.astype(BF16)

    RK = 64
    for c in range(T // RK):
        sl = slice(c * RK, (c + 1) * RK)
        lane = lax.broadcasted_iota(jnp.int32, (RK, LANES), 1)
        lane_lo = (lane % HD_C) < (ROT_DIM // 2)
        head_lo = lane < HD_C
        kr = _rope_slab(pkv_s[sl, 0:LANES], ck_ref[sl, :], sk_ref[sl, :], lane_lo)
        kr_s[sl, :] = kr
        krr = pltpu.roll(kr, HD_C, 1)
        vr = pkv_s[sl, LANES:2 * LANES]
        vrr = pltpu.roll(vr, HD_C, 1)
        dst = slice(W + c * RK, W + (c + 1) * RK)
        zero = jnp.zeros_like(kr)
        kf_s[0, dst, :] = jnp.where(head_lo, kr, zero).astype(BF16)
        kf_s[1, dst, :] = jnp.where(head_lo, zero, krr).astype(BF16)
        kf_s[2, dst, :] = jnp.where(head_lo, krr, zero).astype(BF16)
        kf_s[3, dst, :] = jnp.where(head_lo, zero, kr).astype(BF16)
        vf_s[0, dst, :] = jnp.where(head_lo, vr, zero).astype(BF16)
        vf_s[1, dst, :] = jnp.where(head_lo, zero, vrr).astype(BF16)
        vf_s[2, dst, :] = jnp.where(head_lo, vrr, zero).astype(BF16)
        vf_s[3, dst, :] = jnp.where(head_lo, zero, vr).astype(BF16)

    kj = lax.broadcasted_iota(jnp.int32, (2 * W, W), 0)
    qi = lax.broadcasted_iota(jnp.int32, (2 * W, W), 1)
    band = (kj >= qi) & (kj <= qi + W)
    for i in range(T // W):
        rows = slice(i * W, (i + 1) * W)
        krows = slice(i * W, i * W + 2 * W)
        if i == 0:
            valid = band & ((kj >= W) | (t > 0))
        else:
            valid = band
        bias = jnp.where(valid, 0.0, -jnp.inf)
        for g in range(N_KV_C):
            kk2 = jnp.concatenate([kf_s[2 * g, krows, :], kf_s[2 * g + 1, krows, :]], axis=0)
            vv2 = jnp.concatenate([vf_s[2 * g, krows, :], vf_s[2 * g + 1, krows, :]], axis=0)
            for pp in range(G_C // 2):
                pair = g * (G_C // 2) + pp
                ls = slice(pair * LANES, (pair + 1) * LANES)
                s = _dot(kk2, qt_s[ls, rows])
                probs = []
                for e in range(2):
                    head = 2 * pair + e
                    sh = s[e * 2 * W:(e + 1) * 2 * W, :] + bias
                    sink = sinks_ref[0:1, head:head + 1]
                    m = jnp.maximum(jnp.max(sh, axis=0, keepdims=True), sink)
                    pe = jnp.exp(sh - m)
                    den = jnp.sum(pe, axis=0, keepdims=True) + jnp.exp(sink - m)
                    probs.append((pe * (1.0 / den)).astype(BF16))
                o_s[rows, ls] = lax.dot_general(jnp.concatenate(probs, axis=0), vv2, TN_DIMS,
                                                preferred_element_type=F32)

    def p_gate(r0):
        sl = pl.ds(r0, R)
        g_s[sl, :] = (o_s[sl, :] * _silu(pz_s[sl, :])).astype(BF16)
    _row_loop(T, R, p_gate)

    py_s[...] = _dot(g_s[...], wo_ref[...])
    _rms_scale(T, lambda sl, gs: py_s[sl, gs], st_s, rinv_s, D)

    def p_out(r0):
        sl = pl.ds(r0, R)
        rinv = rinv_s[sl, :]
        for g in range(NG):
            gs = slice(g * LANES, (g + 1) * LANES)
            y_ref[0, sl, gs] = x_ref[0, sl, gs] + py_s[sl, gs] * rinv * gpost_ref[:, gs]
    _row_loop(T, R, p_out)

    @pl.when(t == NT - 1)
    def _():
        kw_ref[0] = kr_s[T - W:T, :]
        vw_ref[0] = pkv_s[T - W:T, LANES:2 * LANES]

    kf_s[:, 0:W, :] = kf_s[:, T:T + W, :]
    vf_s[:, 0:W, :] = vf_s[:, T:T + W, :]


def _odd_prompt(x, ow, tabs, T):
    B, L, D = x.shape
    NT = L // T
    consts = (ow['gpre'], ow['gpost'], ow['wqt'], ow['wkv'], ow['wz'], ow['wo'], ow['sinks'])
    tab_spec = pl.BlockSpec((T, LANES), lambda b, t: (t, 0))
    tabt_spec = pl.BlockSpec((ROT_DIM // 2, T), lambda b, t: (0, t))
    in_specs = ([pl.BlockSpec((1, T, D), lambda b, t: (b, t, 0))] + [_const_spec(c.shape) for c in consts]
                + [tabt_spec] * 2 + [tab_spec] * 2)
    out_shape = (
        jax.ShapeDtypeStruct((B, L, D), F32),
        jax.ShapeDtypeStruct((B, WINDOW, LANES), F32),
        jax.ShapeDtypeStruct((B, WINDOW, LANES), F32),
    )
    out_specs = (
        pl.BlockSpec((1, T, D), lambda b, t: (b, t, 0)),
        pl.BlockSpec((1, WINDOW, LANES), lambda b, t: (b, 0, 0)),
        pl.BlockSpec((1, WINDOW, LANES), lambda b, t: (b, 0, 0)),
    )
    scratch = [
        pltpu.VMEM((T, D), BF16),
        pltpu.VMEM((T, LANES), BF16),
        pltpu.VMEM((T, LANES), F32),
        pltpu.VMEM((T, D), F32),
        pltpu.VMEM((T, D), F32),
        pltpu.VMEM((T, 2 * LANES), F32),
        pltpu.VMEM((D, T), BF16),
        pltpu.VMEM((T, LANES), F32),
        pltpu.VMEM((4, T + WINDOW, LANES), BF16),
        pltpu.VMEM((4, T + WINDOW, LANES), BF16),
        pltpu.VMEM((T, D), F32),
        pltpu.VMEM((T, D), BF16),
    ]
    return pl.pallas_call(
        functools.partial(_odd_prompt_body, T, NT),
        grid=(B, NT),
        in_specs=in_specs,
        out_specs=out_specs,
        out_shape=out_shape,
        scratch_shapes=scratch,
        compiler_params=pltpu.CompilerParams(
            dimension_semantics=("arbitrary", "arbitrary"), vmem_limit_bytes=VMEM_LIMIT),
        name="odd_prompt",
    )(x, *consts, *tabs)


def _sample_front_body(x_ref, cbs_ref, n_ref, mc_ref, mr_ref, gpre_ref, win_ref, wg_ref, wgt_ref, gbr_ref,
                       gbc_ref, cwb_ref, cbb_ref, wq_ref, wk_ref, wkt_ref,
                       anew_ref, az_ref, v_ref, op_ref, zb_ref, cbo_ref, uc_ref, q_ref, ktw_ref,
                       g0_ref, wv_ref, dn_ref, nn_ref, mn_ref):
    D = D_MODEL
    hn = _rms(x_ref[...], gpre_ref[...]).astype(BF16)

    def proj(p):
        return _dot(hn, win_ref[:, p * D:(p + 1) * D])
    anew_ref[...] = proj(0) * _sigmoid(proj(1))
    az_ref[...] = proj(2)
    u = proj(3)
    v_ref[...] = proj(4)
    op_ref[...] = proj(5)
    zb_ref[...] = proj(6)
    gc = _dot(hn, wg_ref[...]) + gbr_ref[...]
    gr = _dot_nt(wgt_ref[...], hn) + gbc_ref[...]

    acc = u * cwb_ref[CONV_B - 1:CONV_B, :]
    for j in range(CONV_B - 1):
        acc = acc + cbs_ref[:, j * D:(j + 1) * D] * cwb_ref[j:j + 1, :]
    uc = _silu(acc + cbb_ref[...])
    uc_ref[...] = uc
    cbo_ref[:, 0:(CONV_B - 2) * D] = cbs_ref[:, D:(CONV_B - 1) * D]
    cbo_ref[:, (CONV_B - 2) * D:] = u

    ucb = uc.astype(BF16)
    g0_ref[...] = jnp.zeros(g0_ref.shape, F32)
    wv_ref[...] = jnp.zeros(wv_ref.shape, F32)
    dn_ref[...] = jnp.ones(dn_ref.shape, F32)
    mn_ref[...] = jnp.zeros(mn_ref.shape, F32)
    for h in range(H_B):
        hs = slice(h * DH_B, (h + 1) * DH_B)
        ub = ucb[:, hs]
        q = _dot(ub, wq_ref[h])
        k = _dot(ub, wk_ref[h]) * K_SCALE_B
        kt = _dot_nt(wkt_ref[h], ub) * K_SCALE_B
        q_ref[:, hs] = q
        nrow = n_ref[:, hs]
        li_c = gc[:, h:h + 1]
        lf_c = _log_sigmoid(gc[:, H_B + h:H_B + h + 1])
        m_c = mc_ref[:, h:h + 1]
        m_new = jnp.maximum(lf_c + m_c, li_c)
        g0 = jnp.exp(lf_c + m_c - m_new)
        ws = jnp.exp(li_c - m_new)
        w = jnp.sum(q * k, axis=1, keepdims=True) * ws
        den = w + g0 * jnp.sum(q * nrow, axis=1, keepdims=True)
        g0_ref[:, h:h + 1] = g0
        wv_ref[:, h:h + 1] = w
        dn_ref[:, h:h + 1] = jnp.maximum(jnp.abs(den), jnp.exp(-m_new))
        mn_ref[:, h:h + 1] = m_new
        nn_ref[:, hs] = g0 * nrow + ws * k
        li_r = gr[h:h + 1, :]
        lf_r = _log_sigmoid(gr[H_B + h:H_B + h + 1, :])
        m_r = mr_ref[h:h + 1, :]
        ws_r = jnp.exp(li_r - jnp.maximum(lf_r + m_r, li_r))
        ktw_ref[h] = (kt * ws_r).astype(BF16)


def _sample_state_body(N, ca_ref, anew_ref, cwa_ref, c_ref, q_ref, v_ref, ktw_ref, g0_ref,
                       cao_ref, co_ref, cn_ref, numi_ref):
    D = D_MODEL
    bb = SAMPLE_BLOCK
    i = pl.program_id(0)
    nst = CONV_A - 1
    a_new = anew_ref[...]
    acc = a_new * cwa_ref[nst:nst + 1, :]
    for r in range(nst):
        acc = acc + ca_ref[:, r * D:(r + 1) * D] * cwa_ref[r:r + 1, :]
    co_ref[...] = acc
    cao_ref[:, 0:(nst - 1) * D] = ca_ref[:, D:nst * D]
    cao_ref[:, (nst - 1) * D:] = a_new

    rown = lax.broadcasted_iota(jnp.int32, (N, DH_B), 0)
    rowb = lax.broadcasted_iota(jnp.int32, (bb, DH_B), 0)
    for h in range(H_B):
        hs = slice(h * DH_B, (h + 1) * DH_B)
        qh = q_ref[:, hs].astype(BF16)
        vh = v_ref[:, hs]
        ktw = ktw_ref[h]
        numi = jnp.zeros((bb, DH_B), F32)
        for j in range(bb):
            cm = c_ref[j, h]
            r = _dot(qh, cm.astype(BF16))
            numi = jnp.where(rowb == j, r, numi)
            vsel = jnp.where(rown == i * bb + j, vh, 0.0).astype(BF16)
            cn_ref[j, h] = g0_ref[j:j + 1, h:h + 1] * cm + _dot(ktw, vsel)
        numi_ref[:, hs] = numi


def _sample_mid_body(NB, x_ref, co_ref, az_ref, numi_ref, v_ref, wv_ref, g0_ref, dn_ref, op_ref, zb_ref, uc_ref,
                     cba_ref, lng_ref, lnb_ref, hng_ref, skip_ref, wout_ref, gpost_ref,
                     gpre_ref, wqs_ref, wkv_ref, wzs_ref, cq_ref, sq_ref, ck_ref, sk_ref,
                     x1_ref, qblk_ref, kn_ref, vn_ref, sz_ref, ycat_s):
    D = D_MODEL
    bb = SAMPLE_BLOCK
    ya = _silu(_layernorm(co_ref[...] + cba_ref[...], lng_ref[...], lnb_ref[...])) * _silu(az_ref[...])
    ycat_s[:, 0:D] = ya.astype(BF16)
    for h in range(H_B):
        hs = slice(h * DH_B, (h + 1) * DH_B)
        num = wv_ref[:, h:h + 1] * v_ref[:, hs] + g0_ref[:, h:h + 1] * numi_ref[:, hs]
        o = _sigmoid(op_ref[:, hs]) * (num / dn_ref[:, h:h + 1])
        hb = _rms(o, hng_ref[:, hs])
        hb = (hb + skip_ref[:, hs] * uc_ref[:, hs]) * _silu(zb_ref[:, hs])
        ycat_s[:, D + h * DH_B:D + (h + 1) * DH_B] = hb.astype(BF16)
    x1 = x_ref[...] + _rms(_dot(ycat_s[...], wout_ref[...]), gpost_ref[...])
    x1_ref[...] = x1

    hn = _rms(x1, gpre_ref[...]).astype(BF16)
    lane = lax.broadcasted_iota(jnp.int32, (x1.shape[0], LANES), 1)
    lane_lo = (lane % HD_C) < (ROT_DIM // 2)
    kv = _dot(hn, wkv_ref[...])
    kn_ref[...] = _rope_slab(kv[:, 0:LANES], ck_ref[...], sk_ref[...], lane_lo)
    vn_ref[...] = kv[:, LANES:2 * LANES]
    sz_ref[...] = _silu(_dot(hn, wzs_ref[...]))
    qs = _dot(hn, wqs_ref[...])
    for h in range(H_C):
        qh = _rope_slab(qs[:, h * LANES:(h + 1) * LANES], cq_ref[...], sq_ref[...], lane_lo)
        for blk in range(NB):
            qblk_ref[blk, h * bb:(h + 1) * bb, :] = qh[blk * bb:(blk + 1) * bb, :]


def _sample_attn_body(q_ref, kc_ref, vc_ref, kn_ref, vn_ref, sink_ref, o_ref, kco_ref, vco_ref):
    bb = SAMPLE_BLOCK
    W = WINDOW
    q = q_ref[0]
    qb = q.astype(BF16)
    rowj = lax.broadcasted_iota(jnp.int32, q.shape, 0) % bb
    sink = sink_ref[:, 0:1]
    acc = jnp.zeros(q.shape, F32)
    for j in range(bb):
        kj = kc_ref[j]
        vj = vc_ref[j]
        kn = kn_ref[j:j + 1, :]
        vn = vn_ref[j:j + 1, :]
        s = _dot_nt(qb, kj.astype(BF16))
        sn = jnp.sum(q * kn, axis=1, keepdims=True)
        m = jnp.maximum(jnp.maximum(jnp.max(s, axis=1, keepdims=True), sn), sink)
        p = jnp.exp(s - m)
        pn = jnp.exp(sn - m)
        den = jnp.sum(p, axis=1, keepdims=True) + pn + jnp.exp(sink - m)
        o = (_dot(p.astype(BF16), vj.astype(BF16)) + pn * vn) / den
        acc = jnp.where(rowj == j, o, acc)
        kco_ref[j, 0:W - 1, :] = kc_ref[j, 1:W, :]
        kco_ref[j, W - 1:W, :] = kn
        vco_ref[j, 0:W - 1, :] = vc_ref[j, 1:W, :]
        vco_ref[j, W - 1:W, :] = vn
    o_ref[0] = acc


def _sample_back_body(NB, o_ref, sz_ref, x1_ref, wos_ref, gpost_ref, y_ref, g_s):
    bb = SAMPLE_BLOCK
    for blk in range(NB):
        rs = slice(blk * bb, (blk + 1) * bb)
        for h in range(H_C):
            ls = slice(h * LANES, (h + 1) * LANES)
            g_s[rs, ls] = o_ref[blk, h * bb:(h + 1) * bb, :] * sz_ref[rs, ls]
    y = _dot(g_s[...].astype(BF16), wos_ref[...])
    y_ref[...] = x1_ref[...] + _rms(y, gpost_ref[...])


def _sample_path(x_s, st_ca, st_cb, st_c, st_n, st_m, ck_win, cv_win, ew, ow, osw):
    N, D = x_s.shape
    bb = SAMPLE_BLOCK
    NB = N // bb
    nst = CONV_A - 1
    f = lambda *shape: jax.ShapeDtypeStruct(shape, F32)
    cparams = pltpu.CompilerParams(vmem_limit_bytes=VMEM_LIMIT)

    m_col = jnp.pad(st_m, ((0, 0), (0, LANES - H_B)))
    m_row = jnp.pad(st_m.T, ((0, 8 - H_B), (0, 0)))
    front_out = (f(N, D), f(N, D), f(N, D), f(N, D), f(N, D), f(N, (CONV_B - 1) * D), f(N, D), f(N, D),
                 jax.ShapeDtypeStruct((H_B, DH_B, N), BF16), f(N, LANES), f(N, LANES), f(N, LANES), f(N, D),
                 f(N, LANES))
    (a_new, az, v, opre, zb, cb_new, uc, q, ktw, g0, wv, dn, n_new, m_new) = pl.pallas_call(
        _sample_front_body, out_shape=front_out, compiler_params=cparams, name="sample_front",
    )(x_s, st_cb.reshape(N, (CONV_B - 1) * D), st_n.reshape(N, D), m_col, m_row,
      ew['gpre'], ew['win'], ew['wg'], ew['wgt'], ew['gbr'], ew['gbc'], ew['cwb'], ew['cbb'],
      ew['wq'], ew['wk'], ew['wkt'])

    blk2 = lambda w: pl.BlockSpec((bb, w), lambda i: (i, 0))
    ca_new, conv_out, c_new, numi = pl.pallas_call(
        functools.partial(_sample_state_body, N),
        grid=(NB,),
        in_specs=[blk2(nst * D), blk2(D), _const_spec(ew['cwa'].shape),
                  pl.BlockSpec((bb, H_B, DH_B, DH_B), lambda i: (i, 0, 0, 0)),
                  blk2(D), _const_spec((N, D)), _const_spec((H_B, DH_B, N)), blk2(LANES)],
        out_specs=(blk2(nst * D), blk2(D), pl.BlockSpec((bb, H_B, DH_B, DH_B), lambda i: (i, 0, 0, 0)), blk2(D)),
        out_shape=(f(N, nst * D), f(N, D), f(N, H_B, DH_B, DH_B), f(N, D)),
        compiler_params=pltpu.CompilerParams(dimension_semantics=("arbitrary",), vmem_limit_bytes=VMEM_LIMIT),
        name="sample_state",
    )(st_ca.reshape(N, nst * D), a_new, ew['cwa'], st_c, q, v, ktw, g0)

    ts = _rope_tables(jnp.full((1,), PAST_LEN, dtype=jnp.int32))
    tabs = (ts['cq'], ts['sq'], ts['ck'], ts['sk'])
    x1, qblk, k_new, v_new, sz = pl.pallas_call(
        functools.partial(_sample_mid_body, NB),
        out_shape=(f(N, D), f(NB, H_C * bb, LANES), f(N, LANES), f(N, LANES), f(N, H_C * LANES)),
        scratch_shapes=[pltpu.VMEM((N, 2 * D), BF16)],
        compiler_params=cparams, name="sample_mid",
    )(x_s, conv_out, az, numi, v, wv, g0, dn, opre, zb, uc,
      ew['cba'], ew['lng'], ew['lnb'], ew['hng'], ew['skip'], ew['wout'], ew['gpost'],
      ow['gpre'], osw['wqs'], ow['wkv'], osw['wzs'], *tabs)

    blk3 = pl.BlockSpec((bb, WINDOW, LANES), lambda i: (i, 0, 0))
    qspec = pl.BlockSpec((1, H_C * bb, LANES), lambda i: (i, 0, 0))
    oblk, kc_new, vc_new = pl.pallas_call(
        _sample_attn_body,
        grid=(NB,),
        in_specs=[qspec, blk3, blk3, blk2(LANES), blk2(LANES), _const_spec((H_C * bb, LANES))],
        out_specs=(qspec, blk3, blk3),
        out_shape=(f(NB, H_C * bb, LANES), f(N, WINDOW, LANES), f(N, WINDOW, LANES)),
        compiler_params=pltpu.CompilerParams(dimension_semantics=("arbitrary",), vmem_limit_bytes=VMEM_LIMIT),
        name="sample_attn",
    )(qblk, ck_win.reshape(N, WINDOW, LANES), cv_win.reshape(N, WINDOW, LANES), k_new, v_new, osw['sinkcol'])

    y_s = pl.pallas_call(
        functools.partial(_sample_back_body, NB),
        out_shape=f(N, D),
        scratch_shapes=[pltpu.VMEM((N, H_C * LANES), F32)],
        compiler_params=cparams, name="sample_back",
    )(oblk, sz, x1, osw['wos'], ow['gpost'])

    return (y_s, ca_new.reshape(N, nst, D), cb_new.reshape(N, CONV_B - 1, D), c_new,
            n_new.reshape(N, H_B, DH_B), m_new[:, :H_B], kc_new, vc_new)


def _prep_even(j, norm_pre_e, norm_post_e, w_in_e, conv_a_w, conv_a_b, ln_a_g, ln_a_b, conv_b_w, conv_b_b,
               wq_b, wk_b, b_i, b_f, headnorm_b, skip_b, w_out_e):
    D = D_MODEL
    w_in = w_in_e[j]
    wgates = w_in[:, 7 * D:]
    gbias = jnp.concatenate([b_i[j], b_f[j]])
    row2 = lambda a: a.reshape(1, -1)
    return dict(
        gpre=row2(norm_pre_e[j]), gpost=row2(norm_post_e[j]),
        win=w_in[:, :7 * D].astype(BF16),
        wg=jnp.pad(wgates, ((0, 0), (0, LANES - 2 * H_B))).astype(BF16),
        wgt=wgates.T.astype(BF16),
        gbr=jnp.pad(gbias, (0, LANES - 2 * H_B)).reshape(1, LANES),
        gbc=gbias.reshape(2 * H_B, 1),
        cwa=conv_a_w[j], cba=row2(conv_a_b[j]), lng=row2(ln_a_g[j]), lnb=row2(ln_a_b[j]),
        cwb=conv_b_w[j], cbb=row2(conv_b_b[j]),
        wq=wq_b[j].astype(BF16), wk=wk_b[j].astype(BF16),
        wkt=jnp.swapaxes(wk_b[j], 1, 2).astype(BF16),
        hng=row2(headnorm_b[j]), skip=row2(skip_b[j]),
        wout=w_out_e[j].astype(BF16),
    )


def _prep_odd(j, norm_pre_o, norm_post_o, w_in_o, sinks, w_out_o):
    D = D_MODEL
    w_in = w_in_o[j]
    row2 = lambda a: a.reshape(1, -1)
    return dict(
        gpre=row2(norm_pre_o[j]), gpost=row2(norm_post_o[j]),
        wqt=(w_in[:, :D].T * Q_SCALE_C).astype(BF16),
        wkv=w_in[:, D:D + 2 * LANES].astype(BF16),
        wz=w_in[:, D + 2 * LANES:].astype(BF16),
        wo=w_out_o[j].astype(BF16),
        sinks=jnp.pad(sinks[j], (0, LANES - H_C)).reshape(1, LANES),
    )


def _prep_odd_sample(j, w_in_o, sinks, w_out_o):
    D = D_MODEL
    w_in = w_in_o[j]
    on_group = (jnp.arange(H_C)[:, None] // G_C) == jnp.arange(N_KV_C)[None, :]

    def stack_cols(w):
        w4 = w.reshape(D, H_C, 1, HD_C)
        return jnp.where(on_group[None, :, :, None], w4, 0.0).reshape(D, H_C * LANES).astype(BF16)
    wo4 = w_out_o[j].reshape(H_C, 1, HD_C, D)
    return dict(
        wqs=stack_cols(w_in[:, :D]),
        wzs=stack_cols(w_in[:, D + 2 * LANES:]),
        wos=jnp.where(on_group[:, :, None, None], wo4, 0.0).reshape(H_C * LANES, D).astype(BF16),
        sinkcol=jnp.broadcast_to(jnp.repeat(sinks[j], SAMPLE_BLOCK)[:, None], (H_C * SAMPLE_BLOCK, LANES)),
    )


def _rope_tables(pos):
    half = ROT_DIM // 2
    inv = jnp.power(ROPE_THETA, -jnp.arange(half, dtype=F32) * (2.0 / ROT_DIM))
    ang = pos.astype(F32)[:, None] * inv[None, :]
    cos, sin = jnp.cos(ang), jnp.sin(ang)
    n = pos.shape[0]
    pad = HD_C - ROT_DIM
    cos64 = jnp.concatenate([cos, cos, jnp.ones((n, pad), F32)], axis=1)
    sin64 = jnp.concatenate([-sin, sin, jnp.zeros((n, pad), F32)], axis=1)
    ck = jnp.concatenate([cos64, cos64], axis=1)
    sk = jnp.concatenate([sin64, sin64], axis=1)
    return dict(cq=ck * Q_SCALE_C, sq=sk * Q_SCALE_C, ck=ck, sk=sk, cos_t=cos.T, sin_t=sin.T)


def kernel(x_prompt, x_sample, state_conv_a, state_conv_b, state_mlstm_c, state_mlstm_n, state_mlstm_m, cache_k_win, cache_v_win, norm_pre_e, norm_post_e, w_in_e, conv_a_w, conv_a_b, ln_a_g, ln_a_b, conv_b_w, conv_b_b, wq_b, wk_b, b_i, b_f, headnorm_b, skip_b, w_out_e, norm_pre_o, norm_post_o, w_in_o, sinks, w_out_o):
    B, L, D = x_prompt.shape
    ew = _prep_even(0, norm_pre_e, norm_post_e, w_in_e, conv_a_w, conv_a_b, ln_a_g, ln_a_b, conv_b_w,
                    conv_b_b, wq_b, wk_b, b_i, b_f, headnorm_b, skip_b, w_out_e)
    ow = _prep_odd(0, norm_pre_o, norm_post_o, w_in_o, sinks, w_out_o)

    x1, ca_p, cb_p, c_p, n_p, m_p = _even_prompt(x_prompt, ew, min(TILE_EVEN, L))
    tp = _rope_tables(jnp.arange(L, dtype=jnp.int32))
    y_p, kw_p, vw_p = _odd_prompt(x1, ow, (tp['cos_t'], tp['sin_t'], tp['ck'], tp['sk']), min(TILE_ODD, L))

    conv_a_p = ca_p[None, :, 32 - (CONV_A - 1):, :]
    conv_b_p = cb_p[None, :, 8 - (CONV_B - 1):, :]
    mlstm_c_p = c_p[None]
    mlstm_n_p = n_p[None]
    mlstm_m_p = m_p[None, :, :H_B, 0]
    k_win_p = kw_p.reshape(1, B, WINDOW, N_KV_C, HD_C)
    v_win_p = vw_p.reshape(1, B, WINDOW, N_KV_C, HD_C)

    N = x_sample.shape[0]
    osw = _prep_odd_sample(0, w_in_o, sinks, w_out_o)
    y_s, ca_s, cb_s, c_s, n_s, m_s, kw_s, vw_s = _sample_path(
        x_sample.reshape(N, D), state_conv_a[0], state_conv_b[0], state_mlstm_c[0], state_mlstm_n[0],
        state_mlstm_m[0], cache_k_win[0], cache_v_win[0], ew, ow, osw)
    return (y_p, y_s.reshape(N, 1, D), conv_a_p, conv_b_p, mlstm_c_p, mlstm_n_p, mlstm_m_p, k_win_p, v_win_p,
            ca_s[None], cb_s[None], c_s[None], n_s[None], m_s[None],
            kw_s.reshape(1, N, WINDOW, N_KV_C, HD_C), vw_s.reshape(1, N, WINDOW, N_KV_C, HD_C))
```

```python
import functools

import jax
import jax.numpy as jnp
from jax import lax
from jax.experimental import pallas as pl
from jax.experimental.pallas import tpu as pltpu

F32 = jnp.float32
BF16 = jnp.bfloat16

D_MODEL = 1024
EPS = 1e-6
PAST_LEN = 16384
CONV_A = 31
CONV_B = 4
H_B = 4
DH_B = D_MODEL // H_B
HD_C = 64
H_C = D_MODEL // HD_C
N_KV_C = 2
G_C = H_C // N_KV_C
WINDOW = 128
ROT_DIM = HD_C // 4
ROPE_THETA = 500000.0
K_SCALE_B = DH_B ** -0.5
Q_SCALE_C = HD_C ** -0.5

LANES = 128
NG = D_MODEL // LANES
STRIDE = 4
TILE_EVEN = 256
TILE_ODD = 256
ROWS = 16
SAMPLE_BLOCK = 8
VMEM_LIMIT = 56 * 1024 * 1024

NT_DIMS = (((1,), (1,)), ((), ()))
TN_DIMS = (((0,), (0,)), ((), ()))


def _dot(a, b):
    return jnp.dot(a, b, preferred_element_type=F32)


def _dot_nt(a, b):
    return lax.dot_general(a, b, NT_DIMS, preferred_element_type=F32)


def _sigmoid(x):
    return 1.0 / (1.0 + jnp.exp(-x))


def _silu(x):
    return x * _sigmoid(x)


def _log_sigmoid(x):
    return jnp.minimum(x, 0.0) - jnp.log1p(jnp.exp(-jnp.abs(x)))


def _rms(x, g):
    ms = jnp.mean(x * x, axis=-1, keepdims=True)
    return x * lax.rsqrt(ms + EPS) * g


def _layernorm(x, g, b):
    mu = jnp.mean(x, axis=-1, keepdims=True)
    xc = x - mu
    var = jnp.mean(xc * xc, axis=-1, keepdims=True)
    return xc * lax.rsqrt(var + EPS) * g + b


def _row_loop(total, rows, body):
    for i in range(total // rows):
        body(i * rows)


def _split3(a):
    hi = a.astype(BF16)
    r1 = a - hi.astype(F32)
    mid = r1.astype(BF16)
    lo = (r1 - mid.astype(F32)).astype(BF16)
    return hi, mid, lo


def _conv_strided(src, w_ref, dst, taps, off, total):
    chunk_rows = 8 * STRIDE

    for c in range(total // chunk_rows):
        base = c * chunk_rows
        for g in range(NG):
            gs = slice(g * LANES, (g + 1) * LANES)
            accs = [None] * STRIDE
            for j in range(taps):
                wj = jnp.broadcast_to(w_ref[j:j + 1, gs], (8, LANES))
                for p in range(STRIDE):
                    v = src[g, pl.ds(base + (p + off + j), 8, stride=STRIDE), :] * wj
                    accs[p] = v if j == 0 else accs[p] + v
            for p in range(STRIDE):
                dst[g, pl.ds(base + p, 8, stride=STRIDE), :] = accs[p]


def _even_prompt_body(T, NT, x_ref, gpre_ref, gpost_ref, win_ref, wg_ref, wgt_ref, gbr_ref, gbc_ref,
                      cwa_ref, cba_ref, lng_ref, lnb_ref, cwb_ref, cbb_ref, wq_ref, wk_ref, wkt_ref,
                      hng_ref, skip_ref, wout_ref,
                      y_ref, ca_ref, cb_ref, c_ref, n_ref, m_ref,
                      hn_s, p_s, abuf, ubuf, cbuf, uc_s, ucb_s, ycat_s, cst_s, nst_s, mst_s,
                      st_s, st2_s, rinv_s, st4_s, rinv4_s):
    D = D_MODEL
    R = ROWS
    t = pl.program_id(1)

    @pl.when(t == 0)
    def _():
        abuf[:, 0:32, :] = jnp.zeros((NG, 32, LANES), F32)
        ubuf[:, 0:8, :] = jnp.zeros((NG, 8, LANES), F32)
        cst_s[...] = jnp.zeros(cst_s.shape, F32)
        nst_s[...] = jnp.zeros(nst_s.shape, F32)
        mst_s[...] = jnp.zeros(mst_s.shape, F32)

    _rms_scale(T, lambda sl, gs: x_ref[0, sl, gs], st_s, rinv_s, D)

    def p_norm(r0):
        sl = pl.ds(r0, R)
        rinv = rinv_s[sl, :]
        for g in range(NG):
            gs = slice(g * LANES, (g + 1) * LANES)
            hn_s[sl, gs] = (x_ref[0, sl, gs] * rinv * gpre_ref[:, gs]).astype(BF16)
    _row_loop(T, R, p_norm)

    hn = hn_s[...]

    def proj(p):
        return _dot(hn, win_ref[:, p * D:(p + 1) * D])
    p_s[0] = proj(0)
    p_s[1] = proj(1)
    p_s[2] = proj(2)
    u_new = proj(3)
    for g in range(NG):
        ubuf[g, 8:8 + T, :] = u_new[:, g * LANES:(g + 1) * LANES]
    p_s[3] = proj(4)
    p_s[4] = proj(5)
    p_s[5] = proj(6)
    gc = _dot(hn, wg_ref[...]) + gbr_ref[...]
    gr = _dot_nt(wgt_ref[...], hn) + gbc_ref[...]

    row = lax.broadcasted_iota(jnp.int32, (T, T), 0)
    col = lax.broadcasted_iota(jnp.int32, (T, T), 1)
    causal = row >= col
    tril = jnp.where(causal, 1.0, 0.0).astype(BF16)
    triu = jnp.where(row <= col, 1.0, 0.0).astype(BF16)
    lf_c = _log_sigmoid(gc)
    lf_r = _log_sigmoid(gr)
    ch, cm, cl = _split3(lf_c)
    b_c = _dot(tril, ch) + _dot(tril, cm) + _dot(tril, cl)
    rh, rm, rl = _split3(lf_r)
    b_r = _dot(rh, triu) + _dot(rm, triu) + _dot(rl, triu)

    def p_glu(r0):
        sl = pl.ds(r0, R)
        for g in range(NG):
            gs = slice(g * LANES, (g + 1) * LANES)
            abuf[g, pl.ds(r0 + 32, R), :] = p_s[0, sl, gs] * _sigmoid(p_s[1, sl, gs])
    _row_loop(T, R, p_glu)

    _conv_strided(abuf, cwa_ref, cbuf, CONV_A, 32 - (CONV_A - 1), T)

    ones = jnp.ones((LANES, LANES), BF16)

    def p_ln_mean(r0):
        sl = pl.ds(r0, R)
        acc = None
        for g in range(NG):
            x = cbuf[g, sl, :] + cba_ref[:, g * LANES:(g + 1) * LANES]
            cbuf[g, sl, :] = x
            acc = x if acc is None else acc + x
        hi = acc.astype(BF16)
        st_s[sl, :] = hi
        st2_s[sl, :] = (acc - hi.astype(F32)).astype(BF16)
    _row_loop(T, R, p_ln_mean)
    rinv_s[...] = (_dot(st_s[...], ones) + _dot(st2_s[...], ones)) * (1.0 / D)

    def p_ln_center(r0):
        sl = pl.ds(r0, R)
        mu = rinv_s[sl, :]
        acc = None
        for g in range(NG):
            xc = cbuf[g, sl, :] - mu
            cbuf[g, sl, :] = xc
            acc = xc * xc if acc is None else acc + xc * xc
        st_s[sl, :] = acc.astype(BF16)
    _row_loop(T, R, p_ln_center)
    rinv_s[...] = lax.rsqrt(_dot(st_s[...], ones) * (1.0 / D) + EPS)

    def p_ln(r0):
        sl = pl.ds(r0, R)
        rs = rinv_s[sl, :]
        for g in range(NG):
            gs = slice(g * LANES, (g + 1) * LANES)
            yv = cbuf[g, sl, :] * rs * lng_ref[:, gs] + lnb_ref[:, gs]
            ycat_s[sl, gs] = (_silu(yv) * _silu(p_s[2, sl, gs])).astype(BF16)
    _row_loop(T, R, p_ln)

    _conv_strided(ubuf, cwb_ref, cbuf, CONV_B, 8 - (CONV_B - 1), T)

    def p_uc(r0):
        sl = pl.ds(r0, R)
        for g in range(NG):
            gs = slice(g * LANES, (g + 1) * LANES)
            uc = _silu(cbuf[g, sl, :] + cbb_ref[:, gs])
            uc_s[sl, gs] = uc
            ucb_s[sl, gs] = uc.astype(BF16)
    _row_loop(T, R, p_uc)

    for h in range(H_B):
        hs = slice(h * DH_B, (h + 1) * DH_B)
        ub = ucb_s[:, hs]
        q = _dot(ub, wq_ref[h])
        k = _dot(ub, wk_ref[h]) * K_SCALE_B
        kt = _dot_nt(wkt_ref[h], ub) * K_SCALE_B
        qb = q.astype(BF16)
        s = _dot(qb, kt.astype(BF16))
        bc = b_c[:, 4 + h:5 + h]
        br = b_r[4 + h:5 + h, :]
        lic = gc[:, h:h + 1]
        lir = gr[h:h + 1, :]
        m_h = mst_s[h:h + 1, 0:1]
        dm = jnp.where(causal, bc - br + lir, -jnp.inf)
        inter = bc + m_h
        mt = jnp.maximum(inter, jnp.max(dm, axis=1, keepdims=True))
        w = s * jnp.exp(dm - mt)
        a_in = jnp.exp(inter - mt)
        vb = p_s[3, :, hs].astype(BF16)
        cmat = cst_s[h]
        nrow = nst_s[h:h + 1, :]
        num = _dot(w.astype(BF16), vb) + a_in * _dot(qb, cmat.astype(BF16))
        den = jnp.sum(w, axis=1, keepdims=True) + a_in * jnp.sum(q * nrow, axis=1, keepdims=True)
        p_s[0, :, hs] = num / jnp.maximum(jnp.abs(den), jnp.exp(-mt))
        bl = bc[T - 1:T, :]
        wl_r = bl - br + lir
        wl_c = bl - bc + lic
        m_new = jnp.maximum(bl + m_h, jnp.max(wl_r, axis=1, keepdims=True))
        g0 = jnp.exp(bl + m_h - m_new)
        ws_r = jnp.exp(wl_r - m_new)
        ws_c = jnp.exp(wl_c - m_new)
        cst_s[h] = g0 * cmat + _dot((kt * ws_r).astype(BF16), vb)
        nst_s[h:h + 1, :] = g0 * nrow + jnp.sum(ws_c * k, axis=0, keepdims=True)
        mst_s[h:h + 1, :] = jnp.broadcast_to(m_new, (1, LANES))

    gph = DH_B // LANES

    def p_ogate(r0):
        sl = pl.ds(r0, R)
        for h in range(H_B):
            acc = None
            for gg in range(gph):
                gs = slice((h * gph + gg) * LANES, (h * gph + gg + 1) * LANES)
                o = _sigmoid(p_s[4, sl, gs]) * p_s[0, sl, gs]
                p_s[0, sl, gs] = o
                acc = o * o if acc is None else acc + o * o
            st4_s[sl, h * LANES:(h + 1) * LANES] = acc.astype(BF16)
    _row_loop(T, R, p_ogate)
    for h in range(H_B):
        hl = slice(h * LANES, (h + 1) * LANES)
        rinv4_s[:, hl] = lax.rsqrt(_dot(st4_s[:, hl], ones) * (1.0 / DH_B) + EPS)

    def p_hb(r0):
        sl = pl.ds(r0, R)
        for g in range(NG):
            gs = slice(g * LANES, (g + 1) * LANES)
            h = g // gph
            hb = p_s[0, sl, gs] * rinv4_s[sl, h * LANES:(h + 1) * LANES] * hng_ref[:, gs]
            hb = (hb + skip_ref[:, gs] * uc_s[sl, gs]) * _silu(p_s[5, sl, gs])
            ycat_s[sl, D + g * LANES:D + (g + 1) * LANES] = hb.astype(BF16)
    _row_loop(T, R, p_hb)

    p_s[1] = _dot(ycat_s[...], wout_ref[...])
    _rms_scale(T, lambda sl, gs: p_s[1, sl, gs], st_s, rinv_s, D)

    def p_out(r0):
        sl = pl.ds(r0, R)
        rinv = rinv_s[sl, :]
        for g in range(NG):
            gs = slice(g * LANES, (g + 1) * LANES)
            y_ref[0, sl, gs] = x_ref[0, sl, gs] + p_s[1, sl, gs] * rinv * gpost_ref[:, gs]
    _row_loop(T, R, p_out)

    @pl.when(t == NT - 1)
    def _():
        for g in range(NG):
            ca_ref[0, :, g * LANES:(g + 1) * LANES] = abuf[g, T:T + 32, :]
            cb_ref[0, :, g * LANES:(g + 1) * LANES] = ubuf[g, T:T + 8, :]
        c_ref[0] = cst_s[...]
        n_ref[0] = nst_s[0:H_B, :]
        m_ref[0] = mst_s[...]

    abuf[:, 0:32, :] = abuf[:, T:T + 32, :]
    ubuf[:, 0:8, :] = ubuf[:, T:T + 8, :]


def _const_spec(shape):
    nd = len(shape)
    return pl.BlockSpec(shape, lambda *_: (0,) * nd, pipeline_mode=pl.Buffered(1))


def _even_prompt(x, ew, T):
    B, L, D = x.shape
    NT = L // T
    consts = (ew['gpre'], ew['gpost'], ew['win'], ew['wg'], ew['wgt'], ew['gbr'], ew['gbc'],
              ew['cwa'], ew['cba'], ew['lng'], ew['lnb'], ew['cwb'], ew['cbb'], ew['wq'], ew['wk'],
              ew['wkt'], ew['hng'], ew['skip'], ew['wout'])
    in_specs = [pl.BlockSpec((1, T, D), lambda b, t: (b, t, 0))] + [_const_spec(c.shape) for c in consts]
    out_shape = (
        jax.ShapeDtypeStruct((B, L, D), F32),
        jax.ShapeDtypeStruct((B, 32, D), F32),
        jax.ShapeDtypeStruct((B, 8, D), F32),
        jax.ShapeDtypeStruct((B, H_B, DH_B, DH_B), F32),
        jax.ShapeDtypeStruct((B, H_B, DH_B), F32),
        jax.ShapeDtypeStruct((B, 8, LANES), F32),
    )
    out_specs = (
        pl.BlockSpec((1, T, D), lambda b, t: (b, t, 0)),
        pl.BlockSpec((1, 32, D), lambda b, t: (b, 0, 0)),
        pl.BlockSpec((1, 8, D), lambda b, t: (b, 0, 0)),
        pl.BlockSpec((1, H_B, DH_B, DH_B), lambda b, t: (b, 0, 0, 0)),
        pl.BlockSpec((1, H_B, DH_B), lambda b, t: (b, 0, 0)),
        pl.BlockSpec((1, 8, LANES), lambda b, t: (b, 0, 0)),
    )
    scratch = [
        pltpu.VMEM((T, D), BF16),
        pltpu.VMEM((6, T, D), F32),
        pltpu.VMEM((NG, T + 32, LANES), F32),
        pltpu.VMEM((NG, T + 8, LANES), F32),
        pltpu.VMEM((NG, T, LANES), F32),
        pltpu.VMEM((T, D), F32),
        pltpu.VMEM((T, D), BF16),
        pltpu.VMEM((T, 2 * D), BF16),
        pltpu.VMEM((H_B, DH_B, DH_B), F32),
        pltpu.VMEM((8, DH_B), F32),
        pltpu.VMEM((8, LANES), F32),
        pltpu.VMEM((T, LANES), BF16),
        pltpu.VMEM((T, LANES), BF16),
        pltpu.VMEM((T, LANES), F32),
        pltpu.VMEM((T, H_B * LANES), BF16),
        pltpu.VMEM((T, H_B * LANES), F32),
    ]
    return pl.pallas_call(
        functools.partial(_even_prompt_body, T, NT),
        grid=(B, NT),
        in_specs=in_specs,
        out_specs=out_specs,
        out_shape=out_shape,
        scratch_shapes=scratch,
        compiler_params=pltpu.CompilerParams(
            dimension_semantics=("arbitrary", "arbitrary"), vmem_limit_bytes=VMEM_LIMIT),
        name="even_prompt",
    )(x, *consts)


def _rope_slab(x, cos, sin, lane_lo):
    partner = jnp.where(lane_lo, pltpu.roll(x, LANES - ROT_DIM // 2, 1), pltpu.roll(x, ROT_DIM // 2, 1))
    return x * cos + partner * sin


def _rms_scale(T, src, st_s, rinv_s, width):
    R = ROWS
    ng = width // LANES

    def p_sq(r0):
        sl = pl.ds(r0, R)
        acc = None
        for g in range(ng):
            xg = src(sl, slice(g * LANES, (g + 1) * LANES))
            acc = xg * xg if acc is None else acc + xg * xg
        st_s[sl, :] = acc.astype(BF16)
    _row_loop(T, R, p_sq)
    ms = _dot(st_s[...], jnp.ones((LANES, LANES), BF16))
    rinv_s[...] = lax.rsqrt(ms * (1.0 / width) + EPS)


def _odd_prompt_body(T, NT, x_ref, gpre_ref, gpost_ref, wqt_ref, wkv_ref, wz_ref, wo_ref, sinks_ref,
                     cqt_ref, sqt_ref, ck_ref, sk_ref,
                     y_ref, kw_ref, vw_ref,
                     hn_s, st_s, rinv_s, py_s, pz_s, pkv_s, qt_s, kr_s, kf_s, vf_s, o_s, g_s):
    D = D_MODEL
    R = ROWS
    W = WINDOW
    t = pl.program_id(1)

    @pl.when(t == 0)
    def _():
        kf_s[:, 0:W, :] = jnp.zeros((4, W, LANES), BF16)
        vf_s[:, 0:W, :] = jnp.zeros((4, W, LANES), BF16)

    _rms_scale(T, lambda sl, gs: x_ref[0, sl, gs], st_s, rinv_s, D)

    def p_norm(r0):
        sl = pl.ds(r0, R)
        rinv = rinv_s[sl, :]
        for g in range(NG):
            gs = slice(g * LANES, (g + 1) * LANES)
            hn_s[sl, gs] = (x_ref[0, sl, gs] * rinv * gpre_ref[:, gs]).astype(BF16)
    _row_loop(T, R, p_norm)

    hn = hn_s[...]
    qt = _dot_nt(wqt_ref[...], hn)
    pkv_s[...] = _dot(hn, wkv_ref[...])
    pz_s[...] = _dot(hn, wz_ref[...])

    cq, sq = cqt_ref[...], sqt_ref[...]
    half = ROT_DIM // 2
    for h in range(H_C):
        b0 = h * HD_C
        x1 = qt[b0:b0 + half, :]
        x2 = qt[b0 + half:b0 + ROT_DIM, :]
        rot = jnp.concatenate([x1 * cq - x2 * sq, x2 * cq + x1 * sq, qt[b0 + ROT_DIM:b0 + HD_C, :]], axis=0)
        qt_s[b0:b0 + HD_C, :] = rot.astype(BF16)

    RK = 64
    for c in range(T // RK):
        sl = slice(c * RK, (c + 1) * RK)
        lane = lax.broadcasted_iota(jnp.int32, (RK, LANES), 1)
        lane_lo = (lane % HD_C) < (ROT_DIM // 2)
        head_lo = lane < HD_C
        kr = _rope_slab(pkv_s[sl, 0:LANES], ck_ref[sl, :], sk_ref[sl, :], lane_lo)
        kr_s[sl, :] = kr
        krr = pltpu.roll(kr, HD_C, 1)
        vr = pkv_s[sl, LANES:2 * LANES]
        vrr = pltpu.roll(vr, HD_C, 1)
        dst = slice(W + c * RK, W + (c + 1) * RK)
        zero = jnp.zeros_like(kr)
        kf_s[0, dst, :] = jnp.where(head_lo, kr, zero).astype(BF16)
        kf_s[1, dst, :] = jnp.where(head_lo, zero, krr).astype(BF16)
        kf_s[2, dst, :] = jnp.where(head_lo, krr, zero).astype(BF16)
        kf_s[3, dst, :] = jnp.where(head_lo, zero, kr).astype(BF16)
        vf_s[0, dst, :] = jnp.where(head_lo, vr, zero).astype(BF16)
        vf_s[1, dst, :] = jnp.where(head_lo, zero, vrr).astype(BF16)
        vf_s[2, dst, :] = jnp.where(head_lo, vrr, zero).astype(BF16)
        vf_s[3, dst, :] = jnp.where(head_lo, zero, vr).astype(BF16)

    kj = lax.broadcasted_iota(jnp.int32, (2 * W, W), 0)
    qi = lax.broadcasted_iota(jnp.int32, (2 * W, W), 1)
    band = (kj >= qi) & (kj <= qi + W)
    for i in range(T // W):
        rows = slice(i * W, (i + 1) * W)
        krows = slice(i * W, i * W + 2 * W)
        if i == 0:
            valid = band & ((kj >= W) | (t > 0))
        else:
            valid = band
        bias = jnp.where(valid, 0.0, -jnp.inf)
        for g in range(N_KV_C):
            kk2 = jnp.concatenate([kf_s[2 * g, krows, :], kf_s[2 * g + 1, krows, :]], axis=0)
            vv2 = jnp.concatenate([vf_s[2 * g, krows, :], vf_s[2 * g + 1, krows, :]], axis=0)
            for pp in range(G_C // 2):
                pair = g * (G_C // 2) + pp
                ls = slice(pair * LANES, (pair + 1) * LANES)
                s = _dot(kk2, qt_s[ls, rows])
                probs = []
                for e in range(2):
                    head = 2 * pair + e
                    sh = s[e * 2 * W:(e + 1) * 2 * W, :] + bias
                    sink = sinks_ref[0:1, head:head + 1]
                    m = jnp.maximum(jnp.max(sh, axis=0, keepdims=True), sink)
                    pe = jnp.exp(sh - m)
                    den = jnp.sum(pe, axis=0, keepdims=True) + jnp.exp(sink - m)
                    probs.append((pe * (1.0 / den)).astype(BF16))
                o_s[rows, ls] = lax.dot_general(jnp.concatenate(probs, axis=0), vv2, TN_DIMS,
                                                preferred_element_type=F32)

    def p_gate(r0):
        sl = pl.ds(r0, R)
        g_s[sl, :] = (o_s[sl, :] * _silu(pz_s[sl, :])).astype(BF16)
    _row_loop(T, R, p_gate)

    py_s[...] = _dot(g_s[...], wo_ref[...])
    _rms_scale(T, lambda sl, gs: py_s[sl, gs], st_s, rinv_s, D)

    def p_out(r0):
        sl = pl.ds(r0, R)
        rinv = rinv_s[sl, :]
        for g in range(NG):
            gs = slice(g * LANES, (g + 1) * LANES)
            y_ref[0, sl, gs] = x_ref[0, sl, gs] + py_s[sl, gs] * rinv * gpost_ref[:, gs]
    _row_loop(T, R, p_out)

    @pl.when(t == NT - 1)
    def _():
        kw_ref[0] = kr_s[T - W:T, :]
        vw_ref[0] = pkv_s[T - W:T, LANES:2 * LANES]

    kf_s[:, 0:W, :] = kf_s[:, T:T + W, :]
    vf_s[:, 0:W, :] = vf_s[:, T:T + W, :]


def _odd_prompt(x, ow, tabs, T):
    B, L, D = x.shape
    NT = L // T
    consts = (ow['gpre'], ow['gpost'], ow['wqt'], ow['wkv'], ow['wz'], ow['wo'], ow['sinks'])
    tab_spec = pl.BlockSpec((T, LANES), lambda b, t: (t, 0))
    tabt_spec = pl.BlockSpec((ROT_DIM // 2, T), lambda b, t: (0, t))
    in_specs = ([pl.BlockSpec((1, T, D), lambda b, t: (b, t, 0))] + [_const_spec(c.shape) for c in consts]
                + [tabt_spec] * 2 + [tab_spec] * 2)
    out_shape = (
        jax.ShapeDtypeStruct((B, L, D), F32),
        jax.ShapeDtypeStruct((B, WINDOW, LANES), F32),
        jax.ShapeDtypeStruct((B, WINDOW, LANES), F32),
    )
    out_specs = (
        pl.BlockSpec((1, T, D), lambda b, t: (b, t, 0)),
        pl.BlockSpec((1, WINDOW, LANES), lambda b, t: (b, 0, 0)),
        pl.BlockSpec((1, WINDOW, LANES), lambda b, t: (b, 0, 0)),
    )
    scratch = [
        pltpu.VMEM((T, D), BF16),
        pltpu.VMEM((T, LANES), BF16),
        pltpu.VMEM((T, LANES), F32),
        pltpu.VMEM((T, D), F32),
        pltpu.VMEM((T, D), F32),
        pltpu.VMEM((T, 2 * LANES), F32),
        pltpu.VMEM((D, T), BF16),
        pltpu.VMEM((T, LANES), F32),
        pltpu.VMEM((4, T + WINDOW, LANES), BF16),
        pltpu.VMEM((4, T + WINDOW, LANES), BF16),
        pltpu.VMEM((T, D), F32),
        pltpu.VMEM((T, D), BF16),
    ]
    return pl.pallas_call(
        functools.partial(_odd_prompt_body, T, NT),
        grid=(B, NT),
        in_specs=in_specs,
        out_specs=out_specs,
        out_shape=out_shape,
        scratch_shapes=scratch,
        compiler_params=pltpu.CompilerParams(
            dimension_semantics=("arbitrary", "arbitrary"), vmem_limit_bytes=VMEM_LIMIT),
        name="odd_prompt",
    )(x, *consts, *tabs)


def _sample_front_body(x_ref, cbs_ref, n_ref, mc_ref, mr_ref, gpre_ref, win_ref, wg_ref, wgt_ref, gbr_ref,
                       gbc_ref, cwb_ref, cbb_ref, wq_ref, wk_ref, wkt_ref,
                       anew_ref, az_ref, v_ref, op_ref, zb_ref, cbo_ref, uc_ref, q_ref, ktw_ref,
                       g0_ref, wv_ref, dn_ref, nn_ref, mn_ref):
    D = D_MODEL
    hn = _rms(x_ref[...], gpre_ref[...]).astype(BF16)

    def proj(p):
        return _dot(hn, win_ref[:, p * D:(p + 1) * D])
    anew_ref[...] = proj(0) * _sigmoid(proj(1))
    az_ref[...] = proj(2)
    u = proj(3)
    v_ref[...] = proj(4)
    op_ref[...] = proj(5)
    zb_ref[...] = proj(6)
    gc = _dot(hn, wg_ref[...]) + gbr_ref[...]
    gr = _dot_nt(wgt_ref[...], hn) + gbc_ref[...]

    acc = u * cwb_ref[CONV_B - 1:CONV_B, :]
    for j in range(CONV_B - 1):
        acc = acc + cbs_ref[:, j * D:(j + 1) * D] * cwb_ref[j:j + 1, :]
    uc = _silu(acc + cbb_ref[...])
    uc_ref[...] = uc
    cbo_ref[:, 0:(CONV_B - 2) * D] = cbs_ref[:, D:(CONV_B - 1) * D]
    cbo_ref[:, (CONV_B - 2) * D:] = u

    ucb = uc.astype(BF16)
    g0_ref[...] = jnp.zeros(g0_ref.shape, F32)
    wv_ref[...] = jnp.zeros(wv_ref.shape, F32)
    dn_ref[...] = jnp.ones(dn_ref.shape, F32)
    mn_ref[...] = jnp.zeros(mn_ref.shape, F32)
    for h in range(H_B):
        hs = slice(h * DH_B, (h + 1) * DH_B)
        ub = ucb[:, hs]
        q = _dot(ub, wq_ref[h])
        k = _dot(ub, wk_ref[h]) * K_SCALE_B
        kt = _dot_nt(wkt_ref[h], ub) * K_SCALE_B
        q_ref[:, hs] = q
        nrow = n_ref[:, hs]
        li_c = gc[:, h:h + 1]
        lf_c = _log_sigmoid(gc[:, H_B + h:H_B + h + 1])
        m_c = mc_ref[:, h:h + 1]
        m_new = jnp.maximum(lf_c + m_c, li_c)
        g0 = jnp.exp(lf_c + m_c - m_new)
        ws = jnp.exp(li_c - m_new)
        w = jnp.sum(q * k, axis=1, keepdims=True) * ws
        den = w + g0 * jnp.sum(q * nrow, axis=1, keepdims=True)
        g0_ref[:, h:h + 1] = g0
        wv_ref[:, h:h + 1] = w
        dn_ref[:, h:h + 1] = jnp.maximum(jnp.abs(den), jnp.exp(-m_new))
        mn_ref[:, h:h + 1] = m_new
        nn_ref[:, hs] = g0 * nrow + ws * k
        li_r = gr[h:h + 1, :]
        lf_r = _log_sigmoid(gr[H_B + h:H_B + h + 1, :])
        m_r = mr_ref[h:h + 1, :]
        ws_r = jnp.exp(li_r - jnp.maximum(lf_r + m_r, li_r))
        ktw_ref[h] = (kt * ws_r).astype(BF16)


def _sample_state_body(N, ca_ref, anew_ref, cwa_ref, c_ref, q_ref, v_ref, ktw_ref, g0_ref,
                       cao_ref, co_ref, cn_ref, numi_ref):
    D = D_MODEL
    bb = SAMPLE_BLOCK
    i = pl.program_id(0)
    nst = CONV_A - 1
    a_new = anew_ref[...]
    acc = a_new * cwa_ref[nst:nst + 1, :]
    for r in range(nst):
        acc = acc + ca_ref[:, r * D:(r + 1) * D] * cwa_ref[r:r + 1, :]
    co_ref[...] = acc
    cao_ref[:, 0:(nst - 1) * D] = ca_ref[:, D:nst * D]
    cao_ref[:, (nst - 1) * D:] = a_new

    rown = lax.broadcasted_iota(jnp.int32, (N, DH_B), 0)
    rowb = lax.broadcasted_iota(jnp.int32, (bb, DH_B), 0)
    for h in range(H_B):
        hs = slice(h * DH_B, (h + 1) * DH_B)
        qh = q_ref[:, hs].astype(BF16)
        vh = v_ref[:, hs]
        ktw = ktw_ref[h]
        numi = jnp.zeros((bb, DH_B), F32)
        for j in range(bb):
            cm = c_ref[j, h]
            r = _dot(qh, cm.astype(BF16))
            numi = jnp.where(rowb == j, r, numi)
            vsel = jnp.where(rown == i * bb + j, vh, 0.0).astype(BF16)
            cn_ref[j, h] = g0_ref[j:j + 1, h:h + 1] * cm + _dot(ktw, vsel)
        numi_ref[:, hs] = numi


def _sample_mid_body(NB, x_ref, co_ref, az_ref, numi_ref, v_ref, wv_ref, g0_ref, dn_ref, op_ref, zb_ref, uc_ref,
                     cba_ref, lng_ref, lnb_ref, hng_ref, skip_ref, wout_ref, gpost_ref,
                     gpre_ref, wqs_ref, wkv_ref, wzs_ref, cq_ref, sq_ref, ck_ref, sk_ref,
                     x1_ref, qblk_ref, kn_ref, vn_ref, sz_ref, ycat_s):
    D = D_MODEL
    bb = SAMPLE_BLOCK
    ya = _silu(_layernorm(co_ref[...] + cba_ref[...], lng_ref[...], lnb_ref[...])) * _silu(az_ref[...])
    ycat_s[:, 0:D] = ya.astype(BF16)
    for h in range(H_B):
        hs = slice(h * DH_B, (h + 1) * DH_B)
        num = wv_ref[:, h:h + 1] * v_ref[:, hs] + g0_ref[:, h:h + 1] * numi_ref[:, hs]
        o = _sigmoid(op_ref[:, hs]) * (num / dn_ref[:, h:h + 1])
        hb = _rms(o, hng_ref[:, hs])
        hb = (hb + skip_ref[:, hs] * uc_ref[:, hs]) * _silu(zb_ref[:, hs])
        ycat_s[:, D + h * DH_B:D + (h + 1) * DH_B] = hb.astype(BF16)
    x1 = x_ref[...] + _rms(_dot(ycat_s[...], wout_ref[...]), gpost_ref[...])
    x1_ref[...] = x1

    hn = _rms(x1, gpre_ref[...]).astype(BF16)
    lane = lax.broadcasted_iota(jnp.int32, (x1.shape[0], LANES), 1)
    lane_lo = (lane % HD_C) < (ROT_DIM // 2)
    kv = _dot(hn, wkv_ref[...])
    kn_ref[...] = _rope_slab(kv[:, 0:LANES], ck_ref[...], sk_ref[...], lane_lo)
    vn_ref[...] = kv[:, LANES:2 * LANES]
    sz_ref[...] = _silu(_dot(hn, wzs_ref[...]))
    qs = _dot(hn, wqs_ref[...])
    for h in range(H_C):
        qh = _rope_slab(qs[:, h * LANES:(h + 1) * LANES], cq_ref[...], sq_ref[...], lane_lo)
        for blk in range(NB):
            qblk_ref[blk, h * bb:(h + 1) * bb, :] = qh[blk * bb:(blk + 1) * bb, :]


def _sample_attn_body(q_ref, kc_ref, vc_ref, kn_ref, vn_ref, sink_ref, o_ref, kco_ref, vco_ref):
    bb = SAMPLE_BLOCK
    W = WINDOW
    q = q_ref[0]
    qb = q.astype(BF16)
    rowj = lax.broadcasted_iota(jnp.int32, q.shape, 0) % bb
    sink = sink_ref[:, 0:1]
    acc = jnp.zeros(q.shape, F32)
    for j in range(bb):
        kj = kc_ref[j]
        vj = vc_ref[j]
        kn = kn_ref[j:j + 1, :]
        vn = vn_ref[j:j + 1, :]
        s = _dot_nt(qb, kj.astype(BF16))
        sn = jnp.sum(q * kn, axis=1, keepdims=True)
        m = jnp.maximum(jnp.maximum(jnp.max(s, axis=1, keepdims=True), sn), sink)
        p = jnp.exp(s - m)
        pn = jnp.exp(sn - m)
        den = jnp.sum(p, axis=1, keepdims=True) + pn + jnp.exp(sink - m)
        o = (_dot(p.astype(BF16), vj.astype(BF16)) + pn * vn) / den
        acc = jnp.where(rowj == j, o, acc)
        kco_ref[j, 0:W - 1, :] = kc_ref[j, 1:W, :]
        kco_ref[j, W - 1:W, :] = kn
        vco_ref[j, 0:W - 1, :] = vc_ref[j, 1:W, :]
        vco_ref[j, W - 1:W, :] = vn
    o_ref[0] = acc


def _sample_back_body(NB, o_ref, sz_ref, x1_ref, wos_ref, gpost_ref, y_ref, g_s):
    bb = SAMPLE_BLOCK
    for blk in range(NB):
        rs = slice(blk * bb, (blk + 1) * bb)
        for h in range(H_C):
            ls = slice(h * LANES, (h + 1) * LANES)
            g_s[rs, ls] = o_ref[blk, h * bb:(h + 1) * bb, :] * sz_ref[rs, ls]
    y = _dot(g_s[...].astype(BF16), wos_ref[...])
    y_ref[...] = x1_ref[...] + _rms(y, gpost_ref[...])


def _sample_path(x_s, st_ca, st_cb, st_c, st_n, st_m, ck_win, cv_win, ew, ow, osw):
    N, D = x_s.shape
    bb = SAMPLE_BLOCK
    NB = N // bb
    nst = CONV_A - 1
    f = lambda *shape: jax.ShapeDtypeStruct(shape, F32)
    cparams = pltpu.CompilerParams(vmem_limit_bytes=VMEM_LIMIT)

    m_col = jnp.pad(st_m, ((0, 0), (0, LANES - H_B)))
    m_row = jnp.pad(st_m.T, ((0, 8 - H_B), (0, 0)))
    front_out = (f(N, D), f(N, D), f(N, D), f(N, D), f(N, D), f(N, (CONV_B - 1) * D), f(N, D), f(N, D),
                 jax.ShapeDtypeStruct((H_B, DH_B, N), BF16), f(N, LANES), f(N, LANES), f(N, LANES), f(N, D),
                 f(N, LANES))
    (a_new, az, v, opre, zb, cb_new, uc, q, ktw, g0, wv, dn, n_new, m_new) = pl.pallas_call(
        _sample_front_body, out_shape=front_out, compiler_params=cparams, name="sample_front",
    )(x_s, st_cb.reshape(N, (CONV_B - 1) * D), st_n.reshape(N, D), m_col, m_row,
      ew['gpre'], ew['win'], ew['wg'], ew['wgt'], ew['gbr'], ew['gbc'], ew['cwb'], ew['cbb'],
      ew['wq'], ew['wk'], ew['wkt'])

    blk2 = lambda w: pl.BlockSpec((bb, w), lambda i: (i, 0))
    ca_new, conv_out, c_new, numi = pl.pallas_call(
        functools.partial(_sample_state_body, N),
        grid=(NB,),
        in_specs=[blk2(nst * D), blk2(D), _const_spec(ew['cwa'].shape),
                  pl.BlockSpec((bb, H_B, DH_B, DH_B), lambda i: (i, 0, 0, 0)),
                  blk2(D), _const_spec((N, D)), _const_spec((H_B, DH_B, N)), blk2(LANES)],
        out_specs=(blk2(nst * D), blk2(D), pl.BlockSpec((bb, H_B, DH_B, DH_B), lambda i: (i, 0, 0, 0)), blk2(D)),
        out_shape=(f(N, nst * D), f(N, D), f(N, H_B, DH_B, DH_B), f(N, D)),
        compiler_params=pltpu.CompilerParams(dimension_semantics=("arbitrary",), vmem_limit_bytes=VMEM_LIMIT),
        name="sample_state",
    )(st_ca.reshape(N, nst * D), a_new, ew['cwa'], st_c, q, v, ktw, g0)

    ts = _rope_tables(jnp.full((1,), PAST_LEN, dtype=jnp.int32))
    tabs = (ts['cq'], ts['sq'], ts['ck'], ts['sk'])
    x1, qblk, k_new, v_new, sz = pl.pallas_call(
        functools.partial(_sample_mid_body, NB),
        out_shape=(f(N, D), f(NB, H_C * bb, LANES), f(N, LANES), f(N, LANES), f(N, H_C * LANES)),
        scratch_shapes=[pltpu.VMEM((N, 2 * D), BF16)],
        compiler_params=cparams, name="sample_mid",
    )(x_s, conv_out, az, numi, v, wv, g0, dn, opre, zb, uc,
      ew['cba'], ew['lng'], ew['lnb'], ew['hng'], ew['skip'], ew['wout'], ew['gpost'],
      ow['gpre'], osw['wqs'], ow['wkv'], osw['wzs'], *tabs)

    blk3 = pl.BlockSpec((bb, WINDOW, LANES), lambda i: (i, 0, 0))
    qspec = pl.BlockSpec((1, H_C * bb, LANES), lambda i: (i, 0, 0))
    oblk, kc_new, vc_new = pl.pallas_call(
        _sample_attn_body,
        grid=(NB,),
        in_specs=[qspec, blk3, blk3, blk2(LANES), blk2(LANES), _const_spec((H_C * bb, LANES))],
        out_specs=(qspec, blk3, blk3),
        out_shape=(f(NB, H_C * bb, LANES), f(N, WINDOW, LANES), f(N, WINDOW, LANES)),
        compiler_params=pltpu.CompilerParams(dimension_semantics=("arbitrary",), vmem_limit_bytes=VMEM_LIMIT),
        name="sample_attn",
    )(qblk, ck_win.reshape(N, WINDOW, LANES), cv_win.reshape(N, WINDOW, LANES), k_new, v_new, osw['sinkcol'])

    y_s = pl.pallas_call(
        functools.partial(_sample_back_body, NB),
        out_shape=f(N, D),
        scratch_shapes=[pltpu.VMEM((N, H_C * LANES), F32)],
        compiler_params=cparams, name="sample_back",
    )(oblk, sz, x1, osw['wos'], ow['gpost'])

    return (y_s, ca_new.reshape(N, nst, D), cb_new.reshape(N, CONV_B - 1, D), c_new,
            n_new.reshape(N, H_B, DH_B), m_new[:, :H_B], kc_new, vc_new)


def _prep_even(j, norm_pre_e, norm_post_e, w_in_e, conv_a_w, conv_a_b, ln_a_g, ln_a_b, conv_b_w, conv_b_b,
               wq_b, wk_b, b_i, b_f, headnorm_b, skip_b, w_out_e):
    D = D_MODEL
    w_in = w_in_e[j]
    wgates = w_in[:, 7 * D:]
    gbias = jnp.concatenate([b_i[j], b_f[j]])
    row2 = lambda a: a.reshape(1, -1)
    return dict(
        gpre=row2(norm_pre_e[j]), gpost=row2(norm_post_e[j]),
        win=w_in[:, :7 * D].astype(BF16),
        wg=jnp.pad(wgates, ((0, 0), (0, LANES - 2 * H_B))).astype(BF16),
        wgt=wgates.T.astype(BF16),
        gbr=jnp.pad(gbias, (0, LANES - 2 * H_B)).reshape(1, LANES),
        gbc=gbias.reshape(2 * H_B, 1),
        cwa=conv_a_w[j], cba=row2(conv_a_b[j]), lng=row2(ln_a_g[j]), lnb=row2(ln_a_b[j]),
        cwb=conv_b_w[j], cbb=row2(conv_b_b[j]),
        wq=wq_b[j].astype(BF16), wk=wk_b[j].astype(BF16),
        wkt=jnp.swapaxes(wk_b[j], 1, 2).astype(BF16),
        hng=row2(headnorm_b[j]), skip=row2(skip_b[j]),
        wout=w_out_e[j].astype(BF16),
    )


def _prep_odd(j, norm_pre_o, norm_post_o, w_in_o, sinks, w_out_o):
    D = D_MODEL
    w_in = w_in_o[j]
    row2 = lambda a: a.reshape(1, -1)
    return dict(
        gpre=row2(norm_pre_o[j]), gpost=row2(norm_post_o[j]),
        wqt=(w_in[:, :D].T * Q_SCALE_C).astype(BF16),
        wkv=w_in[:, D:D + 2 * LANES].astype(BF16),
        wz=w_in[:, D + 2 * LANES:].astype(BF16),
        wo=w_out_o[j].astype(BF16),
        sinks=jnp.pad(sinks[j], (0, LANES - H_C)).reshape(1, LANES),
    )


def _prep_odd_sample(j, w_in_o, sinks, w_out_o):
    D = D_MODEL
    w_in = w_in_o[j]
    on_group = (jnp.arange(H_C)[:, None] // G_C) == jnp.arange(N_KV_C)[None, :]

    def stack_cols(w):
        w4 = w.reshape(D, H_C, 1, HD_C)
        return jnp.where(on_group[None, :, :, None], w4, 0.0).reshape(D, H_C * LANES).astype(BF16)
    wo4 = w_out_o[j].reshape(H_C, 1, HD_C, D)
    return dict(
        wqs=stack_cols(w_in[:, :D]),
        wzs=stack_cols(w_in[:, D + 2 * LANES:]),
        wos=jnp.where(on_group[:, :, None, None], wo4, 0.0).reshape(H_C * LANES, D).astype(BF16),
        sinkcol=jnp.broadcast_to(jnp.repeat(sinks[j], SAMPLE_BLOCK)[:, None], (H_C * SAMPLE_BLOCK, LANES)),
    )


def _rope_tables(pos):
    half = ROT_DIM // 2
    inv = jnp.power(ROPE_THETA, -jnp.arange(half, dtype=F32) * (2.0 / ROT_DIM))
    ang = pos.astype(F32)[:, None] * inv[None, :]
    cos, sin = jnp.cos(ang), jnp.sin(ang)
    n = pos.shape[0]
    pad = HD_C - ROT_DIM
    cos64 = jnp.concatenate([cos, cos, jnp.ones((n, pad), F32)], axis=1)
    sin64 = jnp.concatenate([-sin, sin, jnp.zeros((n, pad), F32)], axis=1)
    ck = jnp.concatenate([cos64, cos64], axis=1)
    sk = jnp.concatenate([sin64, sin64], axis=1)
    return dict(cq=ck * Q_SCALE_C, sq=sk * Q_SCALE_C, ck=ck, sk=sk, cos_t=cos.T, sin_t=sin.T)


def kernel(x_prompt, x_sample, state_conv_a, state_conv_b, state_mlstm_c, state_mlstm_n, state_mlstm_m, cache_k_win, cache_v_win, norm_pre_e, norm_post_e, w_in_e, conv_a_w, conv_a_b, ln_a_g, ln_a_b, conv_b_w, conv_b_b, wq_b, wk_b, b_i, b_f, headnorm_b, skip_b, w_out_e, norm_pre_o, norm_post_o, w_in_o, sinks, w_out_o):
    B, L, D = x_prompt.shape
    ew = _prep_even(0, norm_pre_e, norm_post_e, w_in_e, conv_a_w, conv_a_b, ln_a_g, ln_a_b, conv_b_w,
                    conv_b_b, wq_b, wk_b, b_i, b_f, headnorm_b, skip_b, w_out_e)
    ow = _prep_odd(0, norm_pre_o, norm_post_o, w_in_o, sinks, w_out_o)

    x1, ca_p, cb_p, c_p, n_p, m_p = _even_prompt(x_prompt, ew, min(TILE_EVEN, L))
    tp = _rope_tables(jnp.arange(L, dtype=jnp.int32))
    y_p, kw_p, vw_p = _odd_prompt(x1, ow, (tp['cos_t'], tp['sin_t'], tp['ck'], tp['sk']), min(TILE_ODD, L))

    conv_a_p = ca_p[None, :, 32 - (CONV_A - 1):, :]
    conv_b_p = cb_p[None, :, 8 - (CONV_B - 1):, :]
    mlstm_c_p = c_p[None]
    mlstm_n_p = n_p[None]
    mlstm_m_p = m_p[None, :, :H_B, 0]
    k_win_p = kw_p.reshape(1, B, WINDOW, N_KV_C, HD_C)
    v_win_p = vw_p.reshape(1, B, WINDOW, N_KV_C, HD_C)

    N = x_sample.shape[0]
    osw = _prep_odd_sample(0, w_in_o, sinks, w_out_o)
    y_s, ca_s, cb_s, c_s, n_s, m_s, kw_s, vw_s = _sample_path(
        x_sample.reshape(N, D), state_conv_a[0], state_conv_b[0], state_mlstm_c[0], state_mlstm_n[0],
        state_mlstm_m[0], cache_k_win[0], cache_v_win[0], ew, ow, osw)
    return (y_p, y_s.reshape(N, 1, D), conv_a_p, conv_b_p, mlstm_c_p, mlstm_n_p, mlstm_m_p, k_win_p, v_win_p,
            ca_s[None], cb_s[None], c_s[None], n_s[None], m_s[None],
            kw_s.reshape(1, N, WINDOW, N_KV_C, HD_C), vw_s.reshape(1, N, WINDOW, N_KV_C, HD_C))
```

```python
import functools

import jax
import jax.numpy as jnp
import numpy as np
from jax import lax
from jax.experimental import pallas as pl
from jax.experimental.pallas import tpu as pltpu

F32 = jnp.float32
BF16 = jnp.bfloat16

D_MODEL = 1024
EPS = 1e-6
PAST_LEN = 16384
CONV_A = 31
CONV_B = 4
H_B = 4
DH_B = D_MODEL // H_B
HD_C = 64
H_C = D_MODEL // HD_C
N_KV_C = 2
G_C = H_C // N_KV_C
WINDOW = 128
ROT_DIM = HD_C // 4
ROPE_THETA = 500000.0
K_SCALE_B = DH_B ** -0.5
Q_SCALE_C = HD_C ** -0.5

LANES = 128
NG = D_MODEL // LANES
STRIDE = 4
TILE_EVEN = 256
TILE_ODD = 512
ROWS = 16
AUG = 16
SAMPLE_BLOCK = 8
VMEM_LIMIT = 56 * 1024 * 1024

NT_DIMS = (((1,), (1,)), ((), ()))
TN_DIMS = (((0,), (0,)), ((), ()))


def _dot(a, b):
    return jnp.dot(a, b, preferred_element_type=F32)


def _dot_nt(a, b):
    return lax.dot_general(a, b, NT_DIMS, preferred_element_type=F32)


def _sigmoid(x):
    return 1.0 / (1.0 + jnp.exp(-x))


def _silu(x):
    return x * _sigmoid(x)


def _log_sigmoid(x):
    return jnp.minimum(x, 0.0) - jnp.log1p(jnp.exp(-jnp.abs(x)))


def _rms(x, g):
    ms = jnp.mean(x * x, axis=-1, keepdims=True)
    return x * lax.rsqrt(ms + EPS) * g


def _layernorm(x, g, b):
    mu = jnp.mean(x, axis=-1, keepdims=True)
    xc = x - mu
    var = jnp.mean(xc * xc, axis=-1, keepdims=True)
    return xc * lax.rsqrt(var + EPS) * g + b


def _row_loop(total, rows, body):
    for i in range(total // rows):
        body(i * rows)


def _split3(a):
    hi = a.astype(BF16)
    r1 = a - hi.astype(F32)
    mid = r1.astype(BF16)
    lo = (r1 - mid.astype(F32)).astype(BF16)
    return hi, mid, lo


def _conv_strided(src, w_ref, dst, taps, off, total):
    chunk_rows = 8 * STRIDE

    for c in range(total // chunk_rows):
        base = c * chunk_rows
        for g in range(NG):
            gs = slice(g * LANES, (g + 1) * LANES)
            accs = [None] * STRIDE
            for j in range(taps):
                wj = jnp.broadcast_to(w_ref[j:j + 1, gs], (8, LANES))
                for p in range(STRIDE):
                    v = src[g, pl.ds(base + (p + off + j), 8, stride=STRIDE), :] * wj
                    accs[p] = v if j == 0 else accs[p] + v
            for p in range(STRIDE):
                dst[g, pl.ds(base + p, 8, stride=STRIDE), :] = accs[p]


def _even_prompt_body(T, NT, x_ref, gpre_ref, gpost_ref, win_ref, wvt_ref, wg_ref, gbr_ref,
                      cwa_ref, cba_ref, lng_ref, lnb_ref, cwb_ref, cbb_ref, wqt_ref, wk_ref,
                      hng_ref, skip_ref, wout_ref,
                      y_ref, ca_ref, cb_ref, c_ref, n_ref, m_ref,
                      hn_s, p_s, abuf, ubuf, cbuf, uc_s, ucb_s, ycat_s, cst_s, mst_s,
                      st_s, st2_s, rinv_s, st4_s, rinv4_s, gz_s, vt_s):
    D = D_MODEL
    R = ROWS
    t = pl.program_id(1)

    @pl.when(t == 0)
    def _():
        abuf[:, 0:32, :] = jnp.zeros((NG, 32, LANES), F32)
        ubuf[:, 0:8, :] = jnp.zeros((NG, 8, LANES), F32)
        cst_s[...] = jnp.zeros(cst_s.shape, F32)
        mst_s[...] = jnp.zeros(mst_s.shape, F32)

    _rms_scale(T, lambda sl, gs: x_ref[0, sl, gs], st_s, rinv_s, D)

    def p_norm(r0):
        sl = pl.ds(r0, R)
        rinv = rinv_s[sl, :]
        for g in range(NG):
            gs = slice(g * LANES, (g + 1) * LANES)
            hn_s[sl, gs] = (x_ref[0, sl, gs] * rinv * gpre_ref[:, gs]).astype(BF16)
    _row_loop(T, R, p_norm)

    hn = hn_s[...]

    def proj(p):
        return _dot(hn, win_ref[:, p * D:(p + 1) * D])
    p_s[0] = proj(0)
    p_s[1] = proj(1)
    gz_s[0] = proj(2).astype(BF16)
    u_new = proj(3)
    for g in range(NG):
        ubuf[g, 8:8 + T, :] = u_new[:, g * LANES:(g + 1) * LANES]
    vt_s[...] = _dot_nt(wvt_ref[...], hn).astype(BF16)
    p_s[2] = proj(5)
    gz_s[1] = proj(6).astype(BF16)
    gc = _dot(hn, wg_ref[...]) + gbr_ref[...]

    row = lax.broadcasted_iota(jnp.int32, (T, T), 0)
    col = lax.broadcasted_iota(jnp.int32, (T, T), 1)
    tril = jnp.where(row >= col, 1.0, 0.0).astype(BF16)
    keep = row <= col
    ch, cm, cl = _split3(_log_sigmoid(gc))
    b_c = _dot(tril, ch) + _dot(tril, cm) + _dot(tril, cl)
    b_t = b_c.T

    def p_glu(r0):
        sl = pl.ds(r0, R)
        for g in range(NG):
            gs = slice(g * LANES, (g + 1) * LANES)
            abuf[g, pl.ds(r0 + 32, R), :] = p_s[0, sl, gs] * _sigmoid(p_s[1, sl, gs])
    _row_loop(T, R, p_glu)

    _conv_strided(abuf, cwa_ref, cbuf, CONV_A, 32 - (CONV_A - 1), T)

    ones = jnp.ones((LANES, LANES), BF16)

    def p_ln_mean(r0):
        sl = pl.ds(r0, R)
        acc = None
        for g in range(NG):
            x = cbuf[g, sl, :] + cba_ref[:, g * LANES:(g + 1) * LANES]
            cbuf[g, sl, :] = x
            acc = x if acc is None else acc + x
        hi = acc.astype(BF16)
        st_s[sl, :] = hi
        st2_s[sl, :] = (acc - hi.astype(F32)).astype(BF16)
    _row_loop(T, R, p_ln_mean)
    rinv_s[...] = (_dot(st_s[...], ones) + _dot(st2_s[...], ones)) * (1.0 / D)

    def p_ln_center(r0):
        sl = pl.ds(r0, R)
        mu = rinv_s[sl, :]
        acc = None
        for g in range(NG):
            xc = cbuf[g, sl, :] - mu
            cbuf[g, sl, :] = xc
            acc = xc * xc if acc is None else acc + xc * xc
        st_s[sl, :] = acc.astype(BF16)
    _row_loop(T, R, p_ln_center)
    rinv_s[...] = lax.rsqrt(_dot(st_s[...], ones) * (1.0 / D) + EPS)

    def p_ln(r0):
        sl = pl.ds(r0, R)
        rs = rinv_s[sl, :]
        for g in range(NG):
            gs = slice(g * LANES, (g + 1) * LANES)
            yv = cbuf[g, sl, :] * rs * lng_ref[:, gs] + lnb_ref[:, gs]
            ycat_s[sl, gs] = _silu(yv.astype(BF16)) * _silu(gz_s[0, sl, gs])
    _row_loop(T, R, p_ln)

    _conv_strided(ubuf, cwb_ref, cbuf, CONV_B, 8 - (CONV_B - 1), T)

    def p_uc(r0):
        sl = pl.ds(r0, R)
        for g in range(NG):
            gs = slice(g * LANES, (g + 1) * LANES)
            uc = _silu(cbuf[g, sl, :] + cbb_ref[:, gs])
            uc_s[sl, gs] = uc
            ucb_s[sl, gs] = uc.astype(BF16)
    _row_loop(T, R, p_uc)

    ones_rows = jnp.ones((AUG, T), BF16)
    for h in range(H_B):
        hs = slice(h * DH_B, (h + 1) * DH_B)
        ub = ucb_s[:, hs]
        kb = _dot(ub, wk_ref[h])
        qt = _dot_nt(wqt_ref[h], ub).astype(BF16)
        st = _dot(kb.astype(BF16), qt)
        u_c = gc[:, h:h + 1] - b_c[:, H_B + h:H_B + h + 1]
        b_r = b_t[H_B + h:H_B + h + 1, :]
        m_h = mst_s[h:h + 1, 0:1]
        dm = jnp.where(keep, b_r + u_c, -jnp.inf)
        inter = b_r + m_h
        mt = jnp.maximum(inter, jnp.max(dm, axis=0, keepdims=True))
        wt = (st * jnp.exp(dm - mt)).astype(BF16)
        a_in = jnp.exp(inter - mt)
        vta = jnp.concatenate([vt_s[hs, :], ones_rows], axis=0)
        cta = cst_s[h]
        numt = _dot(vta, wt) + a_in * _dot(cta.astype(BF16), qt)
        den = numt[DH_B:DH_B + 1, :]
        ht = numt[0:DH_B, :] * (1.0 / jnp.maximum(jnp.abs(den), jnp.exp(-mt)))
        p_s[0, :, hs] = ht.T
        bl = b_r[:, T - 1:T]
        wl = bl + u_c
        m_new = jnp.maximum(bl + m_h, jnp.max(wl, axis=0, keepdims=True))
        g0 = jnp.exp(bl + m_h - m_new)
        kw = (kb * jnp.exp(wl - m_new)).astype(BF16)
        cst_s[h] = g0 * cta + _dot(vta, kw)
        mst_s[h:h + 1, :] = jnp.broadcast_to(m_new, (1, LANES))

    gph = DH_B // LANES

    def p_ogate(r0):
        sl = pl.ds(r0, R)
        for h in range(H_B):
            acc = None
            for gg in range(gph):
                gs = slice((h * gph + gg) * LANES, (h * gph + gg + 1) * LANES)
                o = _sigmoid(p_s[2, sl, gs]) * p_s[0, sl, gs]
                p_s[0, sl, gs] = o
                acc = o * o if acc is None else acc + o * o
            st4_s[sl, h * LANES:(h + 1) * LANES] = acc.astype(BF16)
    _row_loop(T, R, p_ogate)
    for h in range(H_B):
        hl = slice(h * LANES, (h + 1) * LANES)
        rinv4_s[:, hl] = lax.rsqrt(_dot(st4_s[:, hl], ones) * (1.0 / DH_B) + EPS)

    def p_hb(r0):
        sl = pl.ds(r0, R)
        for g in range(NG):
            gs = slice(g * LANES, (g + 1) * LANES)
            h = g // gph
            hb = p_s[0, sl, gs] * rinv4_s[sl, h * LANES:(h + 1) * LANES] * hng_ref[:, gs]
            hb = (hb + skip_ref[:, gs] * uc_s[sl, gs]).astype(BF16) * _silu(gz_s[1, sl, gs])
            ycat_s[sl, D + g * LANES:D + (g + 1) * LANES] = hb
    _row_loop(T, R, p_hb)

    p_s[1] = _dot(ycat_s[...], wout_ref[...])
    _rms_scale(T, lambda sl, gs: p_s[1, sl, gs], st_s, rinv_s, D)

    def p_out(r0):
        sl = pl.ds(r0, R)
        rinv = rinv_s[sl, :]
        for g in range(NG):
            gs = slice(g * LANES, (g + 1) * LANES)
            y_ref[0, sl, gs] = x_ref[0, sl, gs] + p_s[1, sl, gs] * rinv * gpost_ref[:, gs]
    _row_loop(T, R, p_out)

    @pl.when(t == NT - 1)
    def _():
        for g in range(NG):
            ca_ref[0, :, g * LANES:(g + 1) * LANES] = abuf[g, T:T + 32, :]
            cb_ref[0, :, g * LANES:(g + 1) * LANES] = ubuf[g, T:T + 8, :]
        for h in range(H_B):
            c_ref[0, h] = cst_s[h, 0:DH_B, :].T
            n_ref[0, h:h + 1, :] = cst_s[h, DH_B:DH_B + 1, :]
        m_ref[0] = mst_s[...]

    abuf[:, 0:32, :] = abuf[:, T:T + 32, :]
    ubuf[:, 0:8, :] = ubuf[:, T:T + 8, :]


def _const_spec(shape):
    nd = len(shape)
    return pl.BlockSpec(shape, lambda *_: (0,) * nd, pipeline_mode=pl.Buffered(1))


def _even_prompt(x, ew, T):
    B, L, D = x.shape
    NT = L // T
    consts = (ew['gpre'], ew['gpost'], ew['win'], ew['wvt'], ew['wg'], ew['gbr'],
              ew['cwa'], ew['cba'], ew['lng'], ew['lnb'], ew['cwb'], ew['cbb'], ew['wqt'], ew['wk'],
              ew['hng'], ew['skip'], ew['wout'])
    in_specs = [pl.BlockSpec((1, T, D), lambda b, t: (b, t, 0))] + [_const_spec(c.shape) for c in consts]
    out_shape = (
        jax.ShapeDtypeStruct((B, L, D), F32),
        jax.ShapeDtypeStruct((B, 32, D), F32),
        jax.ShapeDtypeStruct((B, 8, D), F32),
        jax.ShapeDtypeStruct((B, H_B, DH_B, DH_B), F32),
        jax.ShapeDtypeStruct((B, H_B, DH_B), F32),
        jax.ShapeDtypeStruct((B, 8, LANES), F32),
    )
    out_specs = (
        pl.BlockSpec((1, T, D), lambda b, t: (b, t, 0)),
        pl.BlockSpec((1, 32, D), lambda b, t: (b, 0, 0)),
        pl.BlockSpec((1, 8, D), lambda b, t: (b, 0, 0)),
        pl.BlockSpec((1, H_B, DH_B, DH_B), lambda b, t: (b, 0, 0, 0)),
        pl.BlockSpec((1, H_B, DH_B), lambda b, t: (b, 0, 0)),
        pl.BlockSpec((1, 8, LANES), lambda b, t: (b, 0, 0)),
    )
    scratch = [
        pltpu.VMEM((T, D), BF16),
        pltpu.VMEM((3, T, D), F32),
        pltpu.VMEM((NG, T + 32, LANES), F32),
        pltpu.VMEM((NG, T + 8, LANES), F32),
        pltpu.VMEM((NG, T, LANES), F32),
        pltpu.VMEM((T, D), F32),
        pltpu.VMEM((T, D), BF16),
        pltpu.VMEM((T, 2 * D), BF16),
        pltpu.VMEM((H_B, DH_B + AUG, DH_B), F32),
        pltpu.VMEM((8, LANES), F32),
        pltpu.VMEM((T, LANES), BF16),
        pltpu.VMEM((T, LANES), BF16),
        pltpu.VMEM((T, LANES), F32),
        pltpu.VMEM((T, H_B * LANES), BF16),
        pltpu.VMEM((T, H_B * LANES), F32),
        pltpu.VMEM((2, T, D), BF16),
        pltpu.VMEM((D, T), BF16),
    ]
    return pl.pallas_call(
        functools.partial(_even_prompt_body, T, NT),
        grid=(B, NT),
        in_specs=in_specs,
        out_specs=out_specs,
        out_shape=out_shape,
        scratch_shapes=scratch,
        compiler_params=pltpu.CompilerParams(
            dimension_semantics=("arbitrary", "arbitrary"), vmem_limit_bytes=VMEM_LIMIT),
        name="even_prompt",
    )(x, *consts)


def _rope_slab(x, cos, sin, lane_lo):
    partner = jnp.where(lane_lo, pltpu.roll(x, LANES - ROT_DIM // 2, 1), pltpu.roll(x, ROT_DIM // 2, 1))
    return x * cos + partner * sin


def _rms_scale(T, src, st_s, rinv_s, width):
    R = ROWS
    ng = width // LANES

    def p_sq(r0):
        sl = pl.ds(r0, R)
        acc = None
        for g in range(ng):
            xg = src(sl, slice(g * LANES, (g + 1) * LANES))
            acc = xg * xg if acc is None else acc + xg * xg
        st_s[sl, :] = acc.astype(BF16)
    _row_loop(T, R, p_sq)
    ms = _dot(st_s[...], jnp.ones((LANES, LANES), BF16))
    rinv_s[...] = lax.rsqrt(ms * (1.0 / width) + EPS)


def _odd_prompt_body(T, NT, x_ref, gpre_ref, gpost_ref, wqt_ref, wkv_ref, wz_ref, wo_ref, sinks_ref,
                     cqt_ref, sqt_ref, ck_ref, sk_ref,
                     y_ref, kw_ref, vw_ref,
                     hn_s, st_s, rinv_s, py_s, pz_s, pkv_s, qt_s, kr_s, kb_s, vf_s, o_s, g_s):
    D = D_MODEL
    R = ROWS
    W = WINDOW
    t = pl.program_id(1)

    @pl.when(t == 0)
    def _():
        kb_s[0:W, :] = jnp.zeros((W, LANES), BF16)
        vf_s[:, 0:W, :] = jnp.zeros((4, W, LANES), BF16)

    _rms_scale(T, lambda sl, gs: x_ref[0, sl, gs], st_s, rinv_s, D)

    def p_norm(r0):
        sl = pl.ds(r0, R)
        rinv = rinv_s[sl, :]
        for g in range(NG):
            gs = slice(g * LANES, (g + 1) * LANES)
            hn_s[sl, gs] = (x_ref[0, sl, gs] * rinv * gpre_ref[:, gs]).astype(BF16)
    _row_loop(T, R, p_norm)

    hn = hn_s[...]
    qt = _dot_nt(wqt_ref[...], hn)
    pkv_s[...] = _dot(hn, wkv_ref[...])
    pz_s[...] = _dot(hn, wz_ref[...]).astype(BF16)

    cq, sq = cqt_ref[...], sqt_ref[...]
    half = ROT_DIM // 2
    for h in range(H_C):
        b0 = h * HD_C
        x1 = qt[b0:b0 + half, :]
        x2 = qt[b0 + half:b0 + ROT_DIM, :]
        rot = jnp.concatenate([x1 * cq - x2 * sq, x2 * cq + x1 * sq, qt[b0 + ROT_DIM:b0 + HD_C, :]], axis=0)
        qt_s[b0:b0 + HD_C, :] = rot.astype(BF16)

    RK = 64
    for c in range(T // RK):
        sl = slice(c * RK, (c + 1) * RK)
        lane = lax.broadcasted_iota(jnp.int32, (RK, LANES), 1)
        lane_lo = (lane % HD_C) < (ROT_DIM // 2)
        head_lo = lane < HD_C
        kr = _rope_slab(pkv_s[sl, 0:LANES], ck_ref[sl, :], sk_ref[sl, :], lane_lo)
        kr_s[sl, :] = kr
        vr = pkv_s[sl, LANES:2 * LANES]
        vrr = pltpu.roll(vr, HD_C, 1)
        dst = slice(W + c * RK, W + (c + 1) * RK)
        zero = jnp.zeros_like(kr)
        kb_s[dst, :] = kr.astype(BF16)
        vf_s[0, dst, :] = jnp.where(head_lo, vr, zero).astype(BF16)
        vf_s[1, dst, :] = jnp.where(head_lo, zero, vrr).astype(BF16)
        vf_s[2, dst, :] = jnp.where(head_lo, vrr, zero).astype(BF16)
        vf_s[3, dst, :] = jnp.where(head_lo, zero, vr).astype(BF16)

    kj = lax.broadcasted_iota(jnp.int32, (2 * W, W), 0)
    qi = lax.broadcasted_iota(jnp.int32, (2 * W, W), 1)
    band = (kj >= qi) & (kj <= qi + W)
    for i in range(T // W):
        rows = slice(i * W, (i + 1) * W)
        krows = slice(i * W, i * W + 2 * W)
        if i == 0:
            valid = band & ((kj >= W) | (t > 0))
        else:
            valid = band
        bias = jnp.where(valid, 0.0, -jnp.inf)
        bias2 = jnp.concatenate([bias, bias], axis=1)
        first_head = lax.broadcasted_iota(jnp.int32, (1, 2 * W), 1) < W
        kk = kb_s[krows, :]
        zpad = jnp.zeros((HD_C, 2 * W), BF16)
        for g in range(N_KV_C):
            vv2 = jnp.concatenate([vf_s[2 * g, krows, :], vf_s[2 * g + 1, krows, :]], axis=0)
            for pp in range(G_C // 2):
                pair = g * (G_C // 2) + pp
                ls = slice(pair * LANES, (pair + 1) * LANES)
                qq = jnp.concatenate([qt_s[pair * LANES:pair * LANES + HD_C, rows],
                                      qt_s[pair * LANES + HD_C:(pair + 1) * LANES, rows]], axis=1)
                rhs = jnp.concatenate([qq, zpad] if g == 0 else [zpad, qq], axis=0)
                sh = _dot(kk, rhs) + bias2
                sink = jnp.where(first_head, sinks_ref[0:1, 2 * pair:2 * pair + 1],
                                 sinks_ref[0:1, 2 * pair + 1:2 * pair + 2])
                m = jnp.maximum(jnp.max(sh, axis=0, keepdims=True), sink)
                pe = jnp.exp(sh - m)
                den = jnp.sum(pe, axis=0, keepdims=True) + jnp.exp(sink - m)
                p = (pe * (1.0 / den)).astype(BF16)
                p2t = jnp.concatenate([p[:, 0:W], p[:, W:2 * W]], axis=0)
                o_s[rows, ls] = lax.dot_general(p2t, vv2, TN_DIMS, preferred_element_type=F32)

    def p_gate(r0):
        sl = pl.ds(r0, R)
        g_s[sl, :] = o_s[sl, :].astype(BF16) * _silu(pz_s[sl, :])
    _row_loop(T, R, p_gate)

    py_s[...] = _dot(g_s[...], wo_ref[...])
    _rms_scale(T, lambda sl, gs: py_s[sl, gs], st_s, rinv_s, D)

    def p_out(r0):
        sl = pl.ds(r0, R)
        rinv = rinv_s[sl, :]
        for g in range(NG):
            gs = slice(g * LANES, (g + 1) * LANES)
            y_ref[0, sl, gs] = x_ref[0, sl, gs] + py_s[sl, gs] * rinv * gpost_ref[:, gs]
    _row_loop(T, R, p_out)

    @pl.when(t == NT - 1)
    def _():
        kw_ref[0] = kr_s[T - W:T, :]
        vw_ref[0] = pkv_s[T - W:T, LANES:2 * LANES]

    kb_s[0:W, :] = kb_s[T:T + W, :]
    vf_s[:, 0:W, :] = vf_s[:, T:T + W, :]


def _odd_prompt(x, ow, tabs, T):
    B, L, D = x.shape
    NT = L // T
    consts = (ow['gpre'], ow['gpost'], ow['wqt'], ow['wkv'], ow['wz'], ow['wo'], ow['sinks'])
    tab_spec = pl.BlockSpec((T, LANES), lambda b, t: (t, 0))
    tabt_spec = pl.BlockSpec((ROT_DIM // 2, T), lambda b, t: (0, t))
    in_specs = ([pl.BlockSpec((1, T, D), lambda b, t: (b, t, 0))] + [_const_spec(c.shape) for c in consts]
                + [tabt_spec] * 2 + [tab_spec] * 2)
    out_shape = (
        jax.ShapeDtypeStruct((B, L, D), F32),
        jax.ShapeDtypeStruct((B, WINDOW, LANES), F32),
        jax.ShapeDtypeStruct((B, WINDOW, LANES), F32),
    )
    out_specs = (
        pl.BlockSpec((1, T, D), lambda b, t: (b, t, 0)),
        pl.BlockSpec((1, WINDOW, LANES), lambda b, t: (b, 0, 0)),
        pl.BlockSpec((1, WINDOW, LANES), lambda b, t: (b, 0, 0)),
    )
    scratch = [
        pltpu.VMEM((T, D), BF16),
        pltpu.VMEM((T, LANES), BF16),
        pltpu.VMEM((T, LANES), F32),
        pltpu.VMEM((T, D), F32),
        pltpu.VMEM((T, D), BF16),
        pltpu.VMEM((T, 2 * LANES), F32),
        pltpu.VMEM((D, T), BF16),
        pltpu.VMEM((T, LANES), F32),
        pltpu.VMEM((T + WINDOW, LANES), BF16),
        pltpu.VMEM((4, T + WINDOW, LANES), BF16),
        pltpu.VMEM((T, D), F32),
        pltpu.VMEM((T, D), BF16),
    ]
    return pl.pallas_call(
        functools.partial(_odd_prompt_body, T, NT),
        grid=(B, NT),
        in_specs=in_specs,
        out_specs=out_specs,
        out_shape=out_shape,
        scratch_shapes=scratch,
        compiler_params=pltpu.CompilerParams(
            dimension_semantics=("arbitrary", "arbitrary"), vmem_limit_bytes=VMEM_LIMIT),
        name="odd_prompt",
    )(x, *consts, *tabs)


def _sample_front_body(x_ref, cbs_ref, n_ref, mc_ref, mr_ref, gpre_ref, win_ref, wg_ref, wgt_ref, gbr_ref,
                       gbc_ref, cwb_ref, cbb_ref, wq_ref, wk_ref, wkt_ref,
                       anew_ref, az_ref, v_ref, op_ref, zb_ref, cbo_ref, uc_ref, q_ref, ktw_ref,
                       g0_ref, wv_ref, dn_ref, nn_ref, mn_ref):
    D = D_MODEL
    hn = _rms(x_ref[...], gpre_ref[...]).astype(BF16)

    def proj(p):
        return _dot(hn, win_ref[:, p * D:(p + 1) * D])
    anew_ref[...] = proj(0) * _sigmoid(proj(1))
    az_ref[...] = proj(2)
    u = proj(3)
    v_ref[...] = proj(4)
    op_ref[...] = proj(5)
    zb_ref[...] = proj(6)
    gc = _dot(hn, wg_ref[...]) + gbr_ref[...]
    gr = _dot_nt(wgt_ref[...], hn) + gbc_ref[...]

    acc = u * cwb_ref[CONV_B - 1:CONV_B, :]
    for j in range(CONV_B - 1):
        acc = acc + cbs_ref[:, j * D:(j + 1) * D] * cwb_ref[j:j + 1, :]
    uc = _silu(acc + cbb_ref[...])
    uc_ref[...] = uc
    cbo_ref[:, 0:(CONV_B - 2) * D] = cbs_ref[:, D:(CONV_B - 1) * D]
    cbo_ref[:, (CONV_B - 2) * D:] = u

    ucb = uc.astype(BF16)
    g0_ref[...] = jnp.zeros(g0_ref.shape, F32)
    wv_ref[...] = jnp.zeros(wv_ref.shape, F32)
    dn_ref[...] = jnp.ones(dn_ref.shape, F32)
    mn_ref[...] = jnp.zeros(mn_ref.shape, F32)
    for h in range(H_B):
        hs = slice(h * DH_B, (h + 1) * DH_B)
        ub = ucb[:, hs]
        q = _dot(ub, wq_ref[h])
        k = _dot(ub, wk_ref[h])
        kt = _dot_nt(wkt_ref[h], ub)
        q_ref[:, hs] = q
        nrow = n_ref[:, hs]
        li_c = gc[:, h:h + 1]
        lf_c = _log_sigmoid(gc[:, H_B + h:H_B + h + 1])
        m_c = mc_ref[:, h:h + 1]
        m_new = jnp.maximum(lf_c + m_c, li_c)
        g0 = jnp.exp(lf_c + m_c - m_new)
        ws = jnp.exp(li_c - m_new)
        w = jnp.sum(q * k, axis=1, keepdims=True) * ws
        den = w + g0 * jnp.sum(q * nrow, axis=1, keepdims=True)
        g0_ref[:, h:h + 1] = g0
        wv_ref[:, h:h + 1] = w
        dn_ref[:, h:h + 1] = jnp.maximum(jnp.abs(den), jnp.exp(-m_new))
        mn_ref[:, h:h + 1] = m_new
        nn_ref[:, hs] = g0 * nrow + ws * k
        li_r = gr[h:h + 1, :]
        lf_r = _log_sigmoid(gr[H_B + h:H_B + h + 1, :])
        m_r = mr_ref[h:h + 1, :]
        ws_r = jnp.exp(li_r - jnp.maximum(lf_r + m_r, li_r))
        ktw_ref[h] = (kt * ws_r).astype(BF16)


def _sample_state_body(N, ca_ref, anew_ref, cwa_ref, c_ref, q_ref, v_ref, ktw_ref, g0_ref,
                       cao_ref, co_ref, cn_ref, numi_ref):
    D = D_MODEL
    bb = SAMPLE_BLOCK
    i = pl.program_id(0)
    nst = CONV_A - 1
    a_new = anew_ref[...]
    acc = a_new * cwa_ref[nst:nst + 1, :]
    for r in range(nst):
        acc = acc + ca_ref[:, r * D:(r + 1) * D] * cwa_ref[r:r + 1, :]
    co_ref[...] = acc
    cao_ref[:, 0:(nst - 1) * D] = ca_ref[:, D:nst * D]
    cao_ref[:, (nst - 1) * D:] = a_new

    rown = lax.broadcasted_iota(jnp.int32, (N, DH_B), 0)
    rowb = lax.broadcasted_iota(jnp.int32, (bb, DH_B), 0)
    for h in range(H_B):
        hs = slice(h * DH_B, (h + 1) * DH_B)
        qh = q_ref[:, hs].astype(BF16)
        vh = v_ref[:, hs]
        ktw = ktw_ref[h]
        numi = jnp.zeros((bb, DH_B), F32)
        for j in range(bb):
            cm = c_ref[j, h]
            r = _dot(qh, cm.astype(BF16))
            numi = jnp.where(rowb == j, r, numi)
            vsel = jnp.where(rown == i * bb + j, vh, 0.0).astype(BF16)
            cn_ref[j, h] = g0_ref[j:j + 1, h:h + 1] * cm + _dot(ktw, vsel)
        numi_ref[:, hs] = numi


def _sample_mid_body(NB, x_ref, co_ref, az_ref, numi_ref, v_ref, wv_ref, g0_ref, dn_ref, op_ref, zb_ref, uc_ref,
                     cba_ref, lng_ref, lnb_ref, hng_ref, skip_ref, wout_ref, gpost_ref,
                     gpre_ref, wqs_ref, wkv_ref, wzs_ref, cq_ref, sq_ref, ck_ref, sk_ref,
                     x1_ref, qblk_ref, kn_ref, vn_ref, sz_ref, ycat_s):
    D = D_MODEL
    bb = SAMPLE_BLOCK
    ya = _silu(_layernorm(co_ref[...] + cba_ref[...], lng_ref[...], lnb_ref[...])) * _silu(az_ref[...])
    ycat_s[:, 0:D] = ya.astype(BF16)
    for h in range(H_B):
        hs = slice(h * DH_B, (h + 1) * DH_B)
        num = wv_ref[:, h:h + 1] * v_ref[:, hs] + g0_ref[:, h:h + 1] * numi_ref[:, hs]
        o = _sigmoid(op_ref[:, hs]) * (num / dn_ref[:, h:h + 1])
        hb = _rms(o, hng_ref[:, hs])
        hb = (hb + skip_ref[:, hs] * uc_ref[:, hs]) * _silu(zb_ref[:, hs])
        ycat_s[:, D + h * DH_B:D + (h + 1) * DH_B] = hb.astype(BF16)
    x1 = x_ref[...] + _rms(_dot(ycat_s[...], wout_ref[...]), gpost_ref[...])
    x1_ref[...] = x1

    hn = _rms(x1, gpre_ref[...]).astype(BF16)
    lane = lax.broadcasted_iota(jnp.int32, (x1.shape[0], LANES), 1)
    lane_lo = (lane % HD_C) < (ROT_DIM // 2)
    kv = _dot(hn, wkv_ref[...])
    kn_ref[...] = _rope_slab(kv[:, 0:LANES], ck_ref[...], sk_ref[...], lane_lo)
    vn_ref[...] = kv[:, LANES:2 * LANES]
    sz_ref[...] = _silu(_dot(hn, wzs_ref[...]))
    qs = _dot(hn, wqs_ref[...])
    for h in range(H_C):
        qh = _rope_slab(qs[:, h * LANES:(h + 1) * LANES], cq_ref[...], sq_ref[...], lane_lo)
        for blk in range(NB):
            qblk_ref[blk, h * bb:(h + 1) * bb, :] = qh[blk * bb:(blk + 1) * bb, :]


def _sample_attn_body(q_ref, kc_ref, vc_ref, kn_ref, vn_ref, sink_ref, o_ref, kco_ref, vco_ref):
    bb = SAMPLE_BLOCK
    W = WINDOW
    sink = sink_ref[:, 0:1]
    for j in range(bb):
        q = q_ref[0, pl.ds(j, H_C, stride=bb), :]
        kj = kc_ref[j]
        vj = vc_ref[j]
        kn = kn_ref[j:j + 1, :]
        vn = vn_ref[j:j + 1, :]
        s = _dot_nt(q.astype(BF16), kj.astype(BF16))
        sn = jnp.sum(q * kn, axis=1, keepdims=True)
        m = jnp.maximum(jnp.maximum(jnp.max(s, axis=1, keepdims=True), sn), sink)
        p = jnp.exp(s - m)
        pn = jnp.exp(sn - m)
        den = jnp.sum(p, axis=1, keepdims=True) + pn + jnp.exp(sink - m)
        o = (_dot(p.astype(BF16), vj.astype(BF16)) + pn * vn) / den
        o_ref[0, pl.ds(j, H_C, stride=bb), :] = o
        kco_ref[j, 0:W - 1, :] = kc_ref[j, 1:W, :]
        kco_ref[j, W - 1:W, :] = kn
        vco_ref[j, 0:W - 1, :] = vc_ref[j, 1:W, :]
        vco_ref[j, W - 1:W, :] = vn


def _sample_back_body(NB, o_ref, sz_ref, x1_ref, wos_ref, gpost_ref, y_ref, g_s):
    bb = SAMPLE_BLOCK
    for blk in range(NB):
        rs = slice(blk * bb, (blk + 1) * bb)
        for h in range(H_C):
            ls = slice(h * LANES, (h + 1) * LANES)
            g_s[rs, ls] = o_ref[blk, h * bb:(h + 1) * bb, :] * sz_ref[rs, ls]
    y = _dot(g_s[...].astype(BF16), wos_ref[...])
    y_ref[...] = x1_ref[...] + _rms(y, gpost_ref[...])


def _sample_path(x_s, st_ca, st_cb, st_c, st_n, st_m, ck_win, cv_win, ew, ow, osw):
    N, D = x_s.shape
    bb = SAMPLE_BLOCK
    NB = N // bb
    nst = CONV_A - 1
    f = lambda *shape: jax.ShapeDtypeStruct(shape, F32)
    cparams = pltpu.CompilerParams(vmem_limit_bytes=VMEM_LIMIT)

    m_col = jnp.pad(st_m, ((0, 0), (0, LANES - H_B)))
    m_row = jnp.pad(st_m.T, ((0, 8 - H_B), (0, 0)))
    front_out = (f(N, D), f(N, D), f(N, D), f(N, D), f(N, D), f(N, (CONV_B - 1) * D), f(N, D), f(N, D),
                 jax.ShapeDtypeStruct((H_B, DH_B, N), BF16), f(N, LANES), f(N, LANES), f(N, LANES), f(N, D),
                 f(N, LANES))
    (a_new, az, v, opre, zb, cb_new, uc, q, ktw, g0, wv, dn, n_new, m_new) = pl.pallas_call(
        _sample_front_body, out_shape=front_out, compiler_params=cparams, name="sample_front",
    )(x_s, st_cb.reshape(N, (CONV_B - 1) * D), st_n.reshape(N, D), m_col, m_row,
      ew['gpre'], ew['win'], ew['wg'], ew['wgt'], ew['gbr'], ew['gbc'], ew['cwb'], ew['cbb'],
      ew['wq'], ew['wk'], ew['wkt'])

    blk2 = lambda w: pl.BlockSpec((bb, w), lambda i: (i, 0))
    ca_new, conv_out, c_new, numi = pl.pallas_call(
        functools.partial(_sample_state_body, N),
        grid=(NB,),
        in_specs=[blk2(nst * D), blk2(D), _const_spec(ew['cwa'].shape),
                  pl.BlockSpec((bb, H_B, DH_B, DH_B), lambda i: (i, 0, 0, 0)),
                  blk2(D), _const_spec((N, D)), _const_spec((H_B, DH_B, N)), blk2(LANES)],
        out_specs=(blk2(nst * D), blk2(D), pl.BlockSpec((bb, H_B, DH_B, DH_B), lambda i: (i, 0, 0, 0)), blk2(D)),
        out_shape=(f(N, nst * D), f(N, D), f(N, H_B, DH_B, DH_B), f(N, D)),
        compiler_params=pltpu.CompilerParams(dimension_semantics=("arbitrary",), vmem_limit_bytes=VMEM_LIMIT),
        name="sample_state",
    )(st_ca.reshape(N, nst * D), a_new, ew['cwa'], st_c, q, v, ktw, g0)

    ts = _rope_tables(np.full((1,), PAST_LEN))
    tabs = (ts['cq'], ts['sq'], ts['ck'], ts['sk'])
    x1, qblk, k_new, v_new, sz = pl.pallas_call(
        functools.partial(_sample_mid_body, NB),
        out_shape=(f(N, D), f(NB, H_C * bb, LANES), f(N, LANES), f(N, LANES), f(N, H_C * LANES)),
        scratch_shapes=[pltpu.VMEM((N, 2 * D), BF16)],
        compiler_params=cparams, name="sample_mid",
    )(x_s, conv_out, az, numi, v, wv, g0, dn, opre, zb, uc,
      ew['cba'], ew['lng'], ew['lnb'], ew['hng'], ew['skip'], ew['wout'], ew['gpost'],
      ow['gpre'], osw['wqs'], ow['wkv'], osw['wzs'], *tabs)

    blk3 = pl.BlockSpec((bb, WINDOW, LANES), lambda i: (i, 0, 0))
    qspec = pl.BlockSpec((1, H_C * bb, LANES), lambda i: (i, 0, 0))
    oblk, kc_new, vc_new = pl.pallas_call(
        _sample_attn_body,
        grid=(NB,),
        in_specs=[qspec, blk3, blk3, blk2(LANES), blk2(LANES), _const_spec((H_C, LANES))],
        out_specs=(qspec, blk3, blk3),
        out_shape=(f(NB, H_C * bb, LANES), f(N, WINDOW, LANES), f(N, WINDOW, LANES)),
        compiler_params=pltpu.CompilerParams(dimension_semantics=("arbitrary",), vmem_limit_bytes=VMEM_LIMIT),
        name="sample_attn",
    )(qblk, ck_win.reshape(N, WINDOW, LANES), cv_win.reshape(N, WINDOW, LANES), k_new, v_new, osw['sinkcol'])

    y_s = pl.pallas_call(
        functools.partial(_sample_back_body, NB),
        out_shape=f(N, D),
        scratch_shapes=[pltpu.VMEM((N, H_C * LANES), F32)],
        compiler_params=cparams, name="sample_back",
    )(oblk, sz, x1, osw['wos'], ow['gpost'])

    return (y_s, ca_new.reshape(N, nst, D), cb_new.reshape(N, CONV_B - 1, D), c_new,
            n_new.reshape(N, H_B, DH_B), m_new[:, :H_B], kc_new, vc_new)


def _prep_even(j, norm_pre_e, norm_post_e, w_in_e, conv_a_w, conv_a_b, ln_a_g, ln_a_b, conv_b_w, conv_b_b,
               wq_b, wk_b, b_i, b_f, headnorm_b, skip_b, w_out_e):
    D = D_MODEL
    w_in = w_in_e[j]
    wgates = w_in[:, 7 * D:]
    gbias = jnp.concatenate([b_i[j], b_f[j]])
    row2 = lambda a: a.reshape(1, -1)
    return dict(
        gpre=row2(norm_pre_e[j]), gpost=row2(norm_post_e[j]),
        win=w_in[:, :7 * D].astype(BF16),
        wg=jnp.pad(wgates, ((0, 0), (0, LANES - 2 * H_B))).astype(BF16),
        wgt=wgates.T.astype(BF16),
        gbr=jnp.pad(gbias, (0, LANES - 2 * H_B)).reshape(1, LANES),
        gbc=gbias.reshape(2 * H_B, 1),
        cwa=conv_a_w[j], cba=row2(conv_a_b[j]), lng=row2(ln_a_g[j]), lnb=row2(ln_a_b[j]),
        cwb=conv_b_w[j], cbb=row2(conv_b_b[j]),
        wvt=w_in[:, 4 * D:5 * D].T.astype(BF16),
        wq=wq_b[j].astype(BF16), wqt=jnp.swapaxes(wq_b[j], 1, 2).astype(BF16),
        wk=(wk_b[j] * K_SCALE_B).astype(BF16),
        wkt=(jnp.swapaxes(wk_b[j], 1, 2) * K_SCALE_B).astype(BF16),
        hng=row2(headnorm_b[j]), skip=row2(skip_b[j]),
        wout=w_out_e[j].astype(BF16),
    )


def _prep_odd(j, norm_pre_o, norm_post_o, w_in_o, sinks, w_out_o):
    D = D_MODEL
    w_in = w_in_o[j]
    row2 = lambda a: a.reshape(1, -1)
    return dict(
        gpre=row2(norm_pre_o[j]), gpost=row2(norm_post_o[j]),
        wqt=(w_in[:, :D].T * Q_SCALE_C).astype(BF16),
        wkv=w_in[:, D:D + 2 * LANES].astype(BF16),
        wz=w_in[:, D + 2 * LANES:].astype(BF16),
        wo=w_out_o[j].astype(BF16),
        sinks=jnp.pad(sinks[j], (0, LANES - H_C)).reshape(1, LANES),
    )


def _prep_odd_sample(j, w_in_o, sinks, w_out_o):
    D = D_MODEL
    w_in = w_in_o[j]
    on_group = (jnp.arange(H_C)[:, None] // G_C) == jnp.arange(N_KV_C)[None, :]

    def stack_cols(w):
        w4 = w.reshape(D, H_C, 1, HD_C)
        return jnp.where(on_group[None, :, :, None], w4, 0.0).reshape(D, H_C * LANES).astype(BF16)
    wo4 = w_out_o[j].reshape(H_C, 1, HD_C, D)
    return dict(
        wqs=stack_cols(w_in[:, :D]),
        wzs=stack_cols(w_in[:, D + 2 * LANES:]),
        wos=jnp.where(on_group[:, :, None, None], wo4, 0.0).reshape(H_C * LANES, D).astype(BF16),
        sinkcol=jnp.broadcast_to(sinks[j][:, None], (H_C, LANES)),
    )


def _rope_tables(pos):
    half = ROT_DIM // 2
    f32 = np.float32
    inv = np.power(f32(ROPE_THETA), -np.arange(half, dtype=f32) * f32(2.0 / ROT_DIM)).astype(f32)
    ang = np.asarray(pos, dtype=f32)[:, None] * inv[None, :]
    cos, sin = np.cos(ang).astype(f32), np.sin(ang).astype(f32)
    n = ang.shape[0]
    pad = HD_C - ROT_DIM
    cos64 = np.concatenate([cos, cos, np.ones((n, pad), f32)], axis=1)
    sin64 = np.concatenate([-sin, sin, np.zeros((n, pad), f32)], axis=1)
    ck = np.concatenate([cos64, cos64], axis=1)
    sk = np.concatenate([sin64, sin64], axis=1)
    tabs = dict(cq=ck * f32(Q_SCALE_C), sq=sk * f32(Q_SCALE_C), ck=ck, sk=sk,
                cos_t=np.ascontiguousarray(cos.T), sin_t=np.ascontiguousarray(sin.T))
    return {k: jnp.asarray(v) for k, v in tabs.items()}


def kernel(x_prompt, x_sample, state_conv_a, state_conv_b, state_mlstm_c, state_mlstm_n, state_mlstm_m, cache_k_win, cache_v_win, norm_pre_e, norm_post_e, w_in_e, conv_a_w, conv_a_b, ln_a_g, ln_a_b, conv_b_w, conv_b_b, wq_b, wk_b, b_i, b_f, headnorm_b, skip_b, w_out_e, norm_pre_o, norm_post_o, w_in_o, sinks, w_out_o):
    B, L, D = x_prompt.shape
    ew = _prep_even(0, norm_pre_e, norm_post_e, w_in_e, conv_a_w, conv_a_b, ln_a_g, ln_a_b, conv_b_w,
                    conv_b_b, wq_b, wk_b, b_i, b_f, headnorm_b, skip_b, w_out_e)
    ow = _prep_odd(0, norm_pre_o, norm_post_o, w_in_o, sinks, w_out_o)

    x1, ca_p, cb_p, c_p, n_p, m_p = _even_prompt(x_prompt, ew, min(TILE_EVEN, L))
    tp = _rope_tables(np.arange(L))
    y_p, kw_p, vw_p = _odd_prompt(x1, ow, (tp['cos_t'], tp['sin_t'], tp['ck'], tp['sk']), min(TILE_ODD, L))

    conv_a_p = ca_p[None, :, 32 - (CONV_A - 1):, :]
    conv_b_p = cb_p[None, :, 8 - (CONV_B - 1):, :]
    mlstm_c_p = c_p[None]
    mlstm_n_p = n_p[None]
    mlstm_m_p = m_p[None, :, :H_B, 0]
    k_win_p = kw_p.reshape(1, B, WINDOW, N_KV_C, HD_C)
    v_win_p = vw_p.reshape(1, B, WINDOW, N_KV_C, HD_C)

    N = x_sample.shape[0]
    osw = _prep_odd_sample(0, w_in_o, sinks, w_out_o)
    y_s, ca_s, cb_s, c_s, n_s, m_s, kw_s, vw_s = _sample_path(
        x_sample.reshape(N, D), state_conv_a[0], state_conv_b[0], state_mlstm_c[0], state_mlstm_n[0],
        state_mlstm_m[0], cache_k_win[0], cache_v_win[0], ew, ow, osw)
    return (y_p, y_s.reshape(N, 1, D), conv_a_p, conv_b_p, mlstm_c_p, mlstm_n_p, mlstm_m_p, k_win_p, v_win_p,
            ca_s[None], cb_s[None], c_s[None], n_s[None], m_s[None],
            kw_s.reshape(1, N, WINDOW, N_KV_C, HD_C), vw_s.reshape(1, N, WINDOW, N_KV_C, HD_C))
```

```python
import functools

import jax
import jax.numpy as jnp
import numpy as np
from jax import lax
from jax.experimental import pallas as pl
from jax.experimental.pallas import tpu as pltpu

F32 = jnp.float32
BF16 = jnp.bfloat16

D_MODEL = 1024
EPS = 1e-6
PAST_LEN = 16384
CONV_A = 31
CONV_B = 4
H_B = 4
DH_B = D_MODEL // H_B
HD_C = 64
H_C = D_MODEL // HD_C
N_KV_C = 2
G_C = H_C // N_KV_C
WINDOW = 128
ROT_DIM = HD_C // 4
ROPE_THETA = 500000.0
K_SCALE_B = DH_B ** -0.5
Q_SCALE_C = HD_C ** -0.5

LANES = 128
NG = D_MODEL // LANES
STRIDE = 4
TILE_EVEN = 256
TILE_ODD = 512
ROWS = 16
AUG = 16
SAMPLE_BLOCK = 8
VMEM_LIMIT = 56 * 1024 * 1024

NT_DIMS = (((1,), (1,)), ((), ()))
TT_DIMS = (((0,), (1,)), ((), ()))
TN_DIMS = (((0,), (0,)), ((), ()))


def _dot(a, b):
    return jnp.dot(a, b, preferred_element_type=F32)


def _dot_nt(a, b):
    return lax.dot_general(a, b, NT_DIMS, preferred_element_type=F32)


def _sigmoid(x):
    return 1.0 / (1.0 + jnp.exp(-x))


def _silu(x):
    return x * _sigmoid(x)


def _log_sigmoid(x):
    return jnp.minimum(x, 0.0) - jnp.log1p(jnp.exp(-jnp.abs(x)))


def _rms(x, g):
    ms = jnp.mean(x * x, axis=-1, keepdims=True)
    return x * lax.rsqrt(ms + EPS) * g


def _layernorm(x, g, b):
    mu = jnp.mean(x, axis=-1, keepdims=True)
    xc = x - mu
    var = jnp.mean(xc * xc, axis=-1, keepdims=True)
    return xc * lax.rsqrt(var + EPS) * g + b


def _row_loop(total, rows, body):
    for i in range(total // rows):
        body(i * rows)


def _split3(a):
    hi = a.astype(BF16)
    r1 = a - hi.astype(F32)
    mid = r1.astype(BF16)
    lo = (r1 - mid.astype(F32)).astype(BF16)
    return hi, mid, lo


def _conv_strided(src, w_ref, dst, taps, off, total):
    chunk_rows = 8 * STRIDE

    for c in range(total // chunk_rows):
        base = c * chunk_rows
        for g in range(NG):
            gs = slice(g * LANES, (g + 1) * LANES)
            accs = [None] * STRIDE
            rows = [src[g, pl.ds(base + off + s, 8, stride=STRIDE), :] for s in range(taps + STRIDE - 1)]
            for j in range(taps):
                wj = jnp.broadcast_to(w_ref[j:j + 1, gs], (8, LANES))
                for p in range(STRIDE):
                    v = rows[j + p] * wj
                    accs[p] = v if j == 0 else accs[p] + v
            for p in range(STRIDE):
                dst[g, pl.ds(base + p, 8, stride=STRIDE), :] = accs[p]


def _even_prompt_body(T, NT, x_ref, gpre_ref, gpost_ref, win_ref, wv_ref, wg_ref, gbr_ref,
                      cwa_ref, cba_ref, lng_ref, lnb_ref, cwb_ref, cbb_ref, wqt_ref, wk_ref,
                      hng_ref, skip_ref, wout_ref,
                      y_ref, ca_ref, cb_ref, c_ref, n_ref, m_ref,
                      hn_s, p_s, abuf, ubuf, cbuf, uc_s, ucb_s, ycat_s, cst_s, mst_s,
                      st_s, st2_s, rinv_s, st4_s, rinv4_s, gz_s, vt_s):
    D = D_MODEL
    R = ROWS
    t = pl.program_id(1)

    @pl.when(t == 0)
    def _():
        abuf[:, 0:32, :] = jnp.zeros((NG, 32, LANES), F32)
        ubuf[:, 0:8, :] = jnp.zeros((NG, 8, LANES), F32)
        cst_s[...] = jnp.zeros(cst_s.shape, F32)
        mst_s[...] = jnp.zeros(mst_s.shape, F32)

    _rms_scale(T, lambda sl, gs: x_ref[0, sl, gs], st_s, rinv_s, D)

    def p_norm(r0):
        sl = pl.ds(r0, R)
        rinv = rinv_s[sl, :]
        for g in range(NG):
            gs = slice(g * LANES, (g + 1) * LANES)
            hn_s[sl, gs] = (x_ref[0, sl, gs] * rinv * gpre_ref[:, gs]).astype(BF16)
    _row_loop(T, R, p_norm)

    hn = hn_s[...]

    def proj(p):
        return _dot(hn, win_ref[:, p * D:(p + 1) * D])
    p_s[0] = proj(0)
    p_s[1] = proj(1)
    gz_s[0] = proj(2).astype(BF16)
    u_new = proj(3)
    for g in range(NG):
        ubuf[g, 8:8 + T, :] = u_new[:, g * LANES:(g + 1) * LANES]
    vt_s[...] = lax.dot_general(wv_ref[...], hn, TT_DIMS, preferred_element_type=F32).astype(BF16)
    p_s[2] = proj(4)
    gz_s[1] = proj(5).astype(BF16)
    gc = _dot(hn, wg_ref[...]) + gbr_ref[...]

    row = lax.broadcasted_iota(jnp.int32, (T, T), 0)
    col = lax.broadcasted_iota(jnp.int32, (T, T), 1)
    tril = jnp.where(row >= col, 1.0, 0.0).astype(BF16)
    keep = row <= col
    ch, cm, cl = _split3(_log_sigmoid(gc))
    b_c = _dot(tril, ch) + _dot(tril, cm) + _dot(tril, cl)
    b_t = b_c.T

    def p_glu(r0):
        sl = pl.ds(r0, R)
        for g in range(NG):
            gs = slice(g * LANES, (g + 1) * LANES)
            abuf[g, pl.ds(r0 + 32, R), :] = p_s[0, sl, gs] * _sigmoid(p_s[1, sl, gs])
    _row_loop(T, R, p_glu)

    _conv_strided(abuf, cwa_ref, cbuf, CONV_A, 32 - (CONV_A - 1), T)

    ones = jnp.ones((LANES, LANES), BF16)

    def p_ln_mean(r0):
        sl = pl.ds(r0, R)
        acc = None
        for g in range(NG):
            x = cbuf[g, sl, :] + cba_ref[:, g * LANES:(g + 1) * LANES]
            cbuf[g, sl, :] = x
            acc = x if acc is None else acc + x
        hi = acc.astype(BF16)
        st_s[sl, :] = hi
        st2_s[sl, :] = (acc - hi.astype(F32)).astype(BF16)
    _row_loop(T, R, p_ln_mean)
    rinv_s[...] = (_dot(st_s[...], ones) + _dot(st2_s[...], ones)) * (1.0 / D)

    def p_ln_center(r0):
        sl = pl.ds(r0, R)
        mu = rinv_s[sl, :]
        acc = None
        for g in range(NG):
            xc = cbuf[g, sl, :] - mu
            cbuf[g, sl, :] = xc
            acc = xc * xc if acc is None else acc + xc * xc
        st_s[sl, :] = acc.astype(BF16)
    _row_loop(T, R, p_ln_center)
    rinv_s[...] = lax.rsqrt(_dot(st_s[...], ones) * (1.0 / D) + EPS)

    def p_ln(r0):
        sl = pl.ds(r0, R)
        rs = rinv_s[sl, :]
        for g in range(NG):
            gs = slice(g * LANES, (g + 1) * LANES)
            yv = cbuf[g, sl, :] * rs * lng_ref[:, gs] + lnb_ref[:, gs]
            ycat_s[sl, gs] = _silu(yv.astype(BF16)) * _silu(gz_s[0, sl, gs])
    _row_loop(T, R, p_ln)

    _conv_strided(ubuf, cwb_ref, cbuf, CONV_B, 8 - (CONV_B - 1), T)

    def p_uc(r0):
        sl = pl.ds(r0, R)
        for g in range(NG):
            gs = slice(g * LANES, (g + 1) * LANES)
            uc = _silu(cbuf[g, sl, :] + cbb_ref[:, gs])
            uc_s[sl, gs] = uc
            ucb_s[sl, gs] = uc.astype(BF16)
    _row_loop(T, R, p_uc)

    ones_rows = jnp.ones((AUG, T), BF16)
    for h in range(H_B):
        hs = slice(h * DH_B, (h + 1) * DH_B)
        ub = ucb_s[:, hs]
        kb = _dot(ub, wk_ref[h])
        qt = _dot_nt(wqt_ref[h], ub).astype(BF16)
        st = _dot(kb.astype(BF16), qt)
        u_c = gc[:, h:h + 1] - b_c[:, H_B + h:H_B + h + 1]
        b_r = b_t[H_B + h:H_B + h + 1, :]
        m_h = mst_s[h:h + 1, 0:1]
        dm = jnp.where(keep, b_r + u_c, -jnp.inf)
        inter = b_r + m_h
        mt = jnp.maximum(inter, jnp.max(dm, axis=0, keepdims=True))
        wt = (st * jnp.exp(dm - mt)).astype(BF16)
        a_in = jnp.exp(inter - mt)
        vta = jnp.concatenate([vt_s[hs, :], ones_rows], axis=0)
        cta = cst_s[h]
        numt = _dot(vta, wt) + a_in * _dot(cta.astype(BF16), qt)
        den = numt[DH_B:DH_B + 1, :]
        ht = numt[0:DH_B, :] * (1.0 / jnp.maximum(jnp.abs(den), jnp.exp(-mt)))
        p_s[0, :, hs] = ht.T
        bl = b_r[:, T - 1:T]
        wl = bl + u_c
        m_new = jnp.maximum(bl + m_h, jnp.max(wl, axis=0, keepdims=True))
        g0 = jnp.exp(bl + m_h - m_new)
        kw = (kb * jnp.exp(wl - m_new)).astype(BF16)
        cst_s[h] = g0 * cta + _dot(vta, kw)
        mst_s[h:h + 1, :] = jnp.broadcast_to(m_new, (1, LANES))

    gph = DH_B // LANES

    def p_ogate(r0):
        sl = pl.ds(r0, R)
        for h in range(H_B):
            acc = None
            for gg in range(gph):
                gs = slice((h * gph + gg) * LANES, (h * gph + gg + 1) * LANES)
                o = _sigmoid(p_s[2, sl, gs]) * p_s[0, sl, gs]
                p_s[0, sl, gs] = o
                acc = o * o if acc is None else acc + o * o
            st4_s[sl, h * LANES:(h + 1) * LANES] = acc.astype(BF16)
    _row_loop(T, R, p_ogate)
    for h in range(H_B):
        hl = slice(h * LANES, (h + 1) * LANES)
        rinv4_s[:, hl] = lax.rsqrt(_dot(st4_s[:, hl], ones) * (1.0 / DH_B) + EPS)

    def p_hb(r0):
        sl = pl.ds(r0, R)
        for g in range(NG):
            gs = slice(g * LANES, (g + 1) * LANES)
            h = g // gph
            hb = p_s[0, sl, gs] * rinv4_s[sl, h * LANES:(h + 1) * LANES] * hng_ref[:, gs]
            hb = (hb + skip_ref[:, gs] * uc_s[sl, gs]).astype(BF16) * _silu(gz_s[1, sl, gs])
            ycat_s[sl, D + g * LANES:D + (g + 1) * LANES] = hb
    _row_loop(T, R, p_hb)

    p_s[1] = _dot(ycat_s[...], wout_ref[...])
    _rms_scale(T, lambda sl, gs: p_s[1, sl, gs], st_s, rinv_s, D)

    def p_out(r0):
        sl = pl.ds(r0, R)
        rinv = rinv_s[sl, :]
        for g in range(NG):
            gs = slice(g * LANES, (g + 1) * LANES)
            y_ref[0, sl, gs] = x_ref[0, sl, gs] + p_s[1, sl, gs] * rinv * gpost_ref[:, gs]
    _row_loop(T, R, p_out)

    @pl.when(t == NT - 1)
    def _():
        for g in range(NG):
            ca_ref[0, :, g * LANES:(g + 1) * LANES] = abuf[g, T:T + 32, :]
            cb_ref[0, :, g * LANES:(g + 1) * LANES] = ubuf[g, T:T + 8, :]
        for h in range(H_B):
            c_ref[0, h] = cst_s[h, 0:DH_B, :].T
            n_ref[0, h:h + 1, :] = cst_s[h, DH_B:DH_B + 1, :]
        m_ref[0] = mst_s[...]

    abuf[:, 0:32, :] = abuf[:, T:T + 32, :]
    ubuf[:, 0:8, :] = ubuf[:, T:T + 8, :]


def _const_spec(shape):
    nd = len(shape)
    return pl.BlockSpec(shape, lambda *_: (0,) * nd, pipeline_mode=pl.Buffered(1))


def _even_prompt(x, ew, T):
    B, L, D = x.shape
    NT = L // T
    consts = (ew['gpre'], ew['gpost'], ew['win'], ew['wv'], ew['wg'], ew['gbr'],
              ew['cwa'], ew['cba'], ew['lng'], ew['lnb'], ew['cwb'], ew['cbb'], ew['wqt'], ew['wk'],
              ew['hng'], ew['skip'], ew['wout'])
    in_specs = [pl.BlockSpec((1, T, D), lambda b, t: (b, t, 0))] + [_const_spec(c.shape) for c in consts]
    out_shape = (
        jax.ShapeDtypeStruct((B, L, D), F32),
        jax.ShapeDtypeStruct((B, 32, D), F32),
        jax.ShapeDtypeStruct((B, 8, D), F32),
        jax.ShapeDtypeStruct((B, H_B, DH_B, DH_B), F32),
        jax.ShapeDtypeStruct((B, H_B, DH_B), F32),
        jax.ShapeDtypeStruct((B, 8, LANES), F32),
    )
    out_specs = (
        pl.BlockSpec((1, T, D), lambda b, t: (b, t, 0)),
        pl.BlockSpec((1, 32, D), lambda b, t: (b, 0, 0)),
        pl.BlockSpec((1, 8, D), lambda b, t: (b, 0, 0)),
        pl.BlockSpec((1, H_B, DH_B, DH_B), lambda b, t: (b, 0, 0, 0)),
        pl.BlockSpec((1, H_B, DH_B), lambda b, t: (b, 0, 0)),
        pl.BlockSpec((1, 8, LANES), lambda b, t: (b, 0, 0)),
    )
    scratch = [
        pltpu.VMEM((T, D), BF16),
        pltpu.VMEM((3, T, D), F32),
        pltpu.VMEM((NG, T + 32, LANES), F32),
        pltpu.VMEM((NG, T + 8, LANES), F32),
        pltpu.VMEM((NG, T, LANES), F32),
        pltpu.VMEM((T, D), F32),
        pltpu.VMEM((T, D), BF16),
        pltpu.VMEM((T, 2 * D), BF16),
        pltpu.VMEM((H_B, DH_B + AUG, DH_B), F32),
        pltpu.VMEM((8, LANES), F32),
        pltpu.VMEM((T, LANES), BF16),
        pltpu.VMEM((T, LANES), BF16),
        pltpu.VMEM((T, LANES), F32),
        pltpu.VMEM((T, H_B * LANES), BF16),
        pltpu.VMEM((T, H_B * LANES), F32),
        pltpu.VMEM((2, T, D), BF16),
        pltpu.VMEM((D, T), BF16),
    ]
    return pl.pallas_call(
        functools.partial(_even_prompt_body, T, NT),
        grid=(B, NT),
        in_specs=in_specs,
        out_specs=out_specs,
        out_shape=out_shape,
        scratch_shapes=scratch,
        compiler_params=pltpu.CompilerParams(
            dimension_semantics=("arbitrary", "arbitrary"), vmem_limit_bytes=VMEM_LIMIT),
        name="even_prompt",
    )(x, *consts)


def _rope_slab(x, cos, sin, lane_lo):
    partner = jnp.where(lane_lo, pltpu.roll(x, LANES - ROT_DIM // 2, 1), pltpu.roll(x, ROT_DIM // 2, 1))
    return x * cos + partner * sin


def _rms_scale(T, src, st_s, rinv_s, width):
    R = ROWS
    ng = width // LANES

    def p_sq(r0):
        sl = pl.ds(r0, R)
        acc = None
        for g in range(ng):
            xg = src(sl, slice(g * LANES, (g + 1) * LANES))
            acc = xg * xg if acc is None else acc + xg * xg
        st_s[sl, :] = acc.astype(BF16)
    _row_loop(T, R, p_sq)
    ms = _dot(st_s[...], jnp.ones((LANES, LANES), BF16))
    rinv_s[...] = lax.rsqrt(ms * (1.0 / width) + EPS)


def _odd_prompt_body(T, NT, x_ref, gpre_ref, gpost_ref, wq_ref, wkv_ref, wz_ref, wo_ref, sinks_ref,
                     cqt_ref, sqt_ref, ck_ref, sk_ref,
                     y_ref, kw_ref, vw_ref,
                     hn_s, st_s, rinv_s, py_s, pz_s, pkv_s, qt_s, kr_s, kb_s, vf_s, o_s, g_s):
    D = D_MODEL
    R = ROWS
    W = WINDOW
    t = pl.program_id(1)

    @pl.when(t == 0)
    def _():
        kb_s[0:W, :] = jnp.zeros((W, LANES), BF16)
        vf_s[:, 0:W, :] = jnp.zeros((4, W, LANES), BF16)

    _rms_scale(T, lambda sl, gs: x_ref[0, sl, gs], st_s, rinv_s, D)

    def p_norm(r0):
        sl = pl.ds(r0, R)
        rinv = rinv_s[sl, :]
        for g in range(NG):
            gs = slice(g * LANES, (g + 1) * LANES)
            hn_s[sl, gs] = (x_ref[0, sl, gs] * rinv * gpre_ref[:, gs]).astype(BF16)
    _row_loop(T, R, p_norm)

    hn = hn_s[...]
    qt = lax.dot_general(wq_ref[...], hn, TT_DIMS, preferred_element_type=F32)
    pkv_s[...] = _dot(hn, wkv_ref[...])
    pz_s[...] = _dot(hn, wz_ref[...]).astype(BF16)

    cq, sq = cqt_ref[...], sqt_ref[...]
    half = ROT_DIM // 2
    for h in range(H_C):
        b0 = h * HD_C
        x1 = qt[b0:b0 + half, :]
        x2 = qt[b0 + half:b0 + ROT_DIM, :]
        rot = jnp.concatenate([x1 * cq - x2 * sq, x2 * cq + x1 * sq, qt[b0 + ROT_DIM:b0 + HD_C, :]], axis=0)
        qt_s[b0:b0 + HD_C, :] = rot.astype(BF16)

    RK = 64
    for c in range(T // RK):
        sl = slice(c * RK, (c + 1) * RK)
        lane = lax.broadcasted_iota(jnp.int32, (RK, LANES), 1)
        lane_lo = (lane % HD_C) < (ROT_DIM // 2)
        head_lo = lane < HD_C
        kr = _rope_slab(pkv_s[sl, 0:LANES], ck_ref[sl, :], sk_ref[sl, :], lane_lo)
        kr_s[sl, :] = kr
        vr = pkv_s[sl, LANES:2 * LANES]
        vrr = pltpu.roll(vr, HD_C, 1)
        dst = slice(W + c * RK, W + (c + 1) * RK)
        zero = jnp.zeros_like(kr)
        kb_s[dst, :] = kr.astype(BF16)
        vf_s[0, dst, :] = jnp.where(head_lo, vr, zero).astype(BF16)
        vf_s[1, dst, :] = jnp.where(head_lo, zero, vrr).astype(BF16)
        vf_s[2, dst, :] = jnp.where(head_lo, vrr, zero).astype(BF16)
        vf_s[3, dst, :] = jnp.where(head_lo, zero, vr).astype(BF16)

    kj = lax.broadcasted_iota(jnp.int32, (2 * W, W), 0)
    qi = lax.broadcasted_iota(jnp.int32, (2 * W, W), 1)
    band = (kj >= qi) & (kj <= qi + W)
    for i in range(T // W):
        rows = slice(i * W, (i + 1) * W)
        krows = slice(i * W, i * W + 2 * W)
        if i == 0:
            valid = band & ((kj >= W) | (t > 0))
        else:
            valid = band
        bias = jnp.where(valid, 0.0, -jnp.inf)
        bias2 = jnp.concatenate([bias, bias], axis=1)
        first_head = lax.broadcasted_iota(jnp.int32, (1, 2 * W), 1) < W
        kk = kb_s[krows, :]
        zpad = jnp.zeros((HD_C, 2 * W), BF16)
        for g in range(N_KV_C):
            vv2 = jnp.concatenate([vf_s[2 * g, krows, :], vf_s[2 * g + 1, krows, :]], axis=0)
            for pp in range(G_C // 2):
                pair = g * (G_C // 2) + pp
                ls = slice(pair * LANES, (pair + 1) * LANES)
                qq = jnp.concatenate([qt_s[pair * LANES:pair * LANES + HD_C, rows],
                                      qt_s[pair * LANES + HD_C:(pair + 1) * LANES, rows]], axis=1)
                rhs = jnp.concatenate([qq, zpad] if g == 0 else [zpad, qq], axis=0)
                sh = _dot(kk, rhs) + bias2
                sink = jnp.where(first_head, sinks_ref[0:1, 2 * pair:2 * pair + 1],
                                 sinks_ref[0:1, 2 * pair + 1:2 * pair + 2])
                m = jnp.maximum(jnp.max(sh, axis=0, keepdims=True), sink)
                pe = jnp.exp(sh - m)
                den = jnp.sum(pe, axis=0, keepdims=True) + jnp.exp(sink - m)
                p = (pe * (1.0 / den)).astype(BF16)
                p2t = jnp.concatenate([p[:, 0:W], p[:, W:2 * W]], axis=0)
                o_s[rows, ls] = lax.dot_general(p2t, vv2, TN_DIMS, preferred_element_type=F32)

    def p_gate(r0):
        sl = pl.ds(r0, R)
        g_s[sl, :] = o_s[sl, :].astype(BF16) * _silu(pz_s[sl, :])
    _row_loop(T, R, p_gate)

    py_s[...] = _dot(g_s[...], wo_ref[...])
    _rms_scale(T, lambda sl, gs: py_s[sl, gs], st_s, rinv_s, D)

    def p_out(r0):
        sl = pl.ds(r0, R)
        rinv = rinv_s[sl, :]
        for g in range(NG):
            gs = slice(g * LANES, (g + 1) * LANES)
            y_ref[0, sl, gs] = x_ref[0, sl, gs] + py_s[sl, gs] * rinv * gpost_ref[:, gs]
    _row_loop(T, R, p_out)

    @pl.when(t == NT - 1)
    def _():
        kw_ref[0] = kr_s[T - W:T, :]
        vw_ref[0] = pkv_s[T - W:T, LANES:2 * LANES]

    kb_s[0:W, :] = kb_s[T:T + W, :]
    vf_s[:, 0:W, :] = vf_s[:, T:T + W, :]


def _odd_prompt(x, ow, tabs, T):
    B, L, D = x.shape
    NT = L // T
    consts = (ow['gpre'], ow['gpost'], ow['wq'], ow['wkv'], ow['wz'], ow['wo'], ow['sinks'])
    tab_spec = pl.BlockSpec((T, LANES), lambda b, t: (t, 0))
    tabt_spec = pl.BlockSpec((ROT_DIM // 2, T), lambda b, t: (0, t))
    in_specs = ([pl.BlockSpec((1, T, D), lambda b, t: (b, t, 0))] + [_const_spec(c.shape) for c in consts]
                + [tabt_spec] * 2 + [tab_spec] * 2)
    out_shape = (
        jax.ShapeDtypeStruct((B, L, D), F32),
        jax.ShapeDtypeStruct((B, WINDOW, LANES), F32),
        jax.ShapeDtypeStruct((B, WINDOW, LANES), F32),
    )
    out_specs = (
        pl.BlockSpec((1, T, D), lambda b, t: (b, t, 0)),
        pl.BlockSpec((1, WINDOW, LANES), lambda b, t: (b, 0, 0)),
        pl.BlockSpec((1, WINDOW, LANES), lambda b, t: (b, 0, 0)),
    )
    scratch = [
        pltpu.VMEM((T, D), BF16),
        pltpu.VMEM((T, LANES), BF16),
        pltpu.VMEM((T, LANES), F32),
        pltpu.VMEM((T, D), F32),
        pltpu.VMEM((T, D), BF16),
        pltpu.VMEM((T, 2 * LANES), F32),
        pltpu.VMEM((D, T), BF16),
        pltpu.VMEM((T, LANES), F32),
        pltpu.VMEM((T + WINDOW, LANES), BF16),
        pltpu.VMEM((4, T + WINDOW, LANES), BF16),
        pltpu.VMEM((T, D), F32),
        pltpu.VMEM((T, D), BF16),
    ]
    return pl.pallas_call(
        functools.partial(_odd_prompt_body, T, NT),
        grid=(B, NT),
        in_specs=in_specs,
        out_specs=out_specs,
        out_shape=out_shape,
        scratch_shapes=scratch,
        compiler_params=pltpu.CompilerParams(
            dimension_semantics=("arbitrary", "arbitrary"), vmem_limit_bytes=VMEM_LIMIT),
        name="odd_prompt",
    )(x, *consts, *tabs)


def _sample_front_body(x_ref, cbs_ref, n_ref, mc_ref, mr_ref, gpre_ref, win_ref, wvp_ref, wg_ref, gbr_ref,
                       gbc_ref, cwb_ref, cbb_ref, wq_ref, wk_ref, wkt_ref,
                       anew_ref, az_ref, v_ref, op_ref, zb_ref, cbo_ref, uc_ref, q_ref, ktw_ref,
                       g0_ref, wv_ref, dn_ref, nn_ref, mn_ref):
    D = D_MODEL
    hn = _rms(x_ref[...], gpre_ref[...]).astype(BF16)

    def proj(p):
        return _dot(hn, win_ref[:, p * D:(p + 1) * D])
    anew_ref[...] = proj(0) * _sigmoid(proj(1))
    az_ref[...] = proj(2)
    u = proj(3)
    v_ref[...] = _dot(hn, wvp_ref[...])
    op_ref[...] = proj(4)
    zb_ref[...] = proj(5)
    gc = _dot(hn, wg_ref[...]) + gbr_ref[...]
    gr = lax.dot_general(wg_ref[...], hn, TT_DIMS, preferred_element_type=F32)[0:2 * H_B, :] + gbc_ref[...]

    acc = u * cwb_ref[CONV_B - 1:CONV_B, :]
    for j in range(CONV_B - 1):
        acc = acc + cbs_ref[:, j * D:(j + 1) * D] * cwb_ref[j:j + 1, :]
    uc = _silu(acc + cbb_ref[...])
    uc_ref[...] = uc
    cbo_ref[:, 0:(CONV_B - 2) * D] = cbs_ref[:, D:(CONV_B - 1) * D]
    cbo_ref[:, (CONV_B - 2) * D:] = u

    ucb = uc.astype(BF16)
    g0_ref[...] = jnp.zeros(g0_ref.shape, F32)
    wv_ref[...] = jnp.zeros(wv_ref.shape, F32)
    dn_ref[...] = jnp.ones(dn_ref.shape, F32)
    mn_ref[...] = jnp.zeros(mn_ref.shape, F32)
    for h in range(H_B):
        hs = slice(h * DH_B, (h + 1) * DH_B)
        ub = ucb[:, hs]
        q = _dot(ub, wq_ref[h])
        k = _dot(ub, wk_ref[h])
        kt = _dot_nt(wkt_ref[h], ub)
        q_ref[:, hs] = q
        nrow = n_ref[:, hs]
        li_c = gc[:, h:h + 1]
        lf_c = _log_sigmoid(gc[:, H_B + h:H_B + h + 1])
        m_c = mc_ref[:, h:h + 1]
        m_new = jnp.maximum(lf_c + m_c, li_c)
        g0 = jnp.exp(lf_c + m_c - m_new)
        ws = jnp.exp(li_c - m_new)
        w = jnp.sum(q * k, axis=1, keepdims=True) * ws
        den = w + g0 * jnp.sum(q * nrow, axis=1, keepdims=True)
        g0_ref[:, h:h + 1] = g0
        wv_ref[:, h:h + 1] = w
        dn_ref[:, h:h + 1] = jnp.maximum(jnp.abs(den), jnp.exp(-m_new))
        mn_ref[:, h:h + 1] = m_new
        nn_ref[:, hs] = g0 * nrow + ws * k
        li_r = gr[h:h + 1, :]
        lf_r = _log_sigmoid(gr[H_B + h:H_B + h + 1, :])
        m_r = mr_ref[h:h + 1, :]
        ws_r = jnp.exp(li_r - jnp.maximum(lf_r + m_r, li_r))
        ktw_ref[h] = (kt * ws_r).astype(BF16)


def _sample_state_body(N, ca_ref, anew_ref, cwa_ref, c_ref, q_ref, v_ref, ktw_ref, g0_ref,
                       cao_ref, co_ref, cn_ref, numi_ref):
    D = D_MODEL
    bb = SAMPLE_BLOCK
    i = pl.program_id(0)
    nst = CONV_A - 1
    for j in range(bb):
        a_new = anew_ref[j:j + 1, :]
        past = jnp.sum(ca_ref[0, j] * cwa_ref[0:nst, :], axis=0, keepdims=True)
        co_ref[j:j + 1, :] = past + a_new * cwa_ref[nst:nst + 1, :]
        cao_ref[0, j, 0:nst - 1, :] = ca_ref[0, j, 1:nst, :]
        cao_ref[0, j, nst - 1:nst, :] = a_new

    rown = lax.broadcasted_iota(jnp.int32, (N, DH_B), 0)
    rowb = lax.broadcasted_iota(jnp.int32, (bb, DH_B), 0)
    for h in range(H_B):
        hs = slice(h * DH_B, (h + 1) * DH_B)
        qh = q_ref[:, hs].astype(BF16)
        vh = v_ref[:, hs]
        ktw = ktw_ref[h]
        numi = jnp.zeros((bb, DH_B), F32)
        for j in range(bb):
            cm = c_ref[j, h]
            r = _dot(qh, cm.astype(BF16))
            numi = jnp.where(rowb == j, r, numi)
            vsel = jnp.where(rown == i * bb + j, vh, 0.0).astype(BF16)
            cn_ref[j, h] = g0_ref[j:j + 1, h:h + 1] * cm + _dot(ktw, vsel)
        numi_ref[:, hs] = numi


def _sample_mid_body(NB, x_ref, co_ref, az_ref, numi_ref, v_ref, wv_ref, g0_ref, dn_ref, op_ref, zb_ref, uc_ref,
                     cba_ref, lng_ref, lnb_ref, hng_ref, skip_ref, wout_ref, gpost_ref,
                     gpre_ref, wqs_ref, wkv_ref, wzs_ref, cq_ref, sq_ref, ck_ref, sk_ref,
                     x1_ref, qblk_ref, kn_ref, vn_ref, sz_ref, ycat_s):
    D = D_MODEL
    bb = SAMPLE_BLOCK
    ya = _silu(_layernorm(co_ref[...] + cba_ref[...], lng_ref[...], lnb_ref[...])) * _silu(az_ref[...])
    ycat_s[:, 0:D] = ya.astype(BF16)
    for h in range(H_B):
        hs = slice(h * DH_B, (h + 1) * DH_B)
        num = wv_ref[:, h:h + 1] * v_ref[:, hs] + g0_ref[:, h:h + 1] * numi_ref[:, hs]
        o = _sigmoid(op_ref[:, hs]) * (num / dn_ref[:, h:h + 1])
        hb = _rms(o, hng_ref[:, hs])
        hb = (hb + skip_ref[:, hs] * uc_ref[:, hs]) * _silu(zb_ref[:, hs])
        ycat_s[:, D + h * DH_B:D + (h + 1) * DH_B] = hb.astype(BF16)
    x1 = x_ref[...] + _rms(_dot(ycat_s[...], wout_ref[...]), gpost_ref[...])
    x1_ref[...] = x1

    hn = _rms(x1, gpre_ref[...]).astype(BF16)
    lane = lax.broadcasted_iota(jnp.int32, (x1.shape[0], LANES), 1)
    lane_lo = (lane % HD_C) < (ROT_DIM // 2)
    kv = _dot(hn, wkv_ref[...])
    kn_ref[...] = _rope_slab(kv[:, 0:LANES], ck_ref[...], sk_ref[...], lane_lo)
    vn_ref[...] = kv[:, LANES:2 * LANES]
    sz_ref[...] = _silu(_dot(hn, wzs_ref[...]))
    qs = _dot(hn, wqs_ref[...])
    for h in range(H_C):
        qh = _rope_slab(qs[:, h * LANES:(h + 1) * LANES], cq_ref[...], sq_ref[...], lane_lo)
        for blk in range(NB):
            qblk_ref[blk, h * bb:(h + 1) * bb, :] = qh[blk * bb:(blk + 1) * bb, :]


def _sample_attn_body(q_ref, kc_ref, vc_ref, kn_ref, vn_ref, sink_ref, o_ref, kco_ref, vco_ref):
    bb = SAMPLE_BLOCK
    W = WINDOW
    qs = [q_ref[0, pl.ds(j, H_C, stride=bb), :] for j in range(bb)]
    s = jnp.concatenate([_dot_nt(qs[j].astype(BF16), kc_ref[j].astype(BF16)) for j in range(bb)], axis=0)
    sn = jnp.concatenate([jnp.sum(qs[j] * kn_ref[j:j + 1, :], axis=1, keepdims=True) for j in range(bb)], axis=0)
    sink = jnp.concatenate([sink_ref[:, 0:1]] * bb, axis=0)
    m = jnp.maximum(jnp.maximum(jnp.max(s, axis=1, keepdims=True), sn), sink)
    p = jnp.exp(s - m)
    pn = jnp.exp(sn - m)
    rden = 1.0 / (jnp.sum(p, axis=1, keepdims=True) + pn + jnp.exp(sink - m))
    pb = (p * rden).astype(BF16)
    pn = pn * rden
    for j in range(bb):
        rows = slice(j * H_C, (j + 1) * H_C)
        kn = kn_ref[j:j + 1, :]
        vn = vn_ref[j:j + 1, :]
        o_ref[0, pl.ds(j, H_C, stride=bb), :] = _dot(pb[rows, :], vc_ref[j].astype(BF16)) + pn[rows, :] * vn
        kco_ref[j, 0:W - 1, :] = kc_ref[j, 1:W, :]
        kco_ref[j, W - 1:W, :] = kn
        vco_ref[j, 0:W - 1, :] = vc_ref[j, 1:W, :]
        vco_ref[j, W - 1:W, :] = vn


def _sample_back_body(NB, o_ref, sz_ref, x1_ref, wos_ref, gpost_ref, y_ref, g_s):
    bb = SAMPLE_BLOCK
    for blk in range(NB):
        rs = slice(blk * bb, (blk + 1) * bb)
        for h in range(H_C):
            ls = slice(h * LANES, (h + 1) * LANES)
            g_s[rs, ls] = o_ref[blk, h * bb:(h + 1) * bb, :] * sz_ref[rs, ls]
    y = _dot(g_s[...].astype(BF16), wos_ref[...])
    y_ref[...] = x1_ref[...] + _rms(y, gpost_ref[...])


def _sample_path(x_s, st_ca, st_cb, st_c, st_n, st_m, ck_win, cv_win, ew, ow, osw):
    N, D = x_s.shape
    bb = SAMPLE_BLOCK
    NB = N // bb
    nst = CONV_A - 1
    f = lambda *shape: jax.ShapeDtypeStruct(shape, F32)
    cparams = pltpu.CompilerParams(vmem_limit_bytes=VMEM_LIMIT)

    m_col = jnp.pad(st_m, ((0, 0), (0, LANES - H_B)))
    m_row = jnp.pad(st_m.T, ((0, 8 - H_B), (0, 0)))
    front_out = (f(N, D), f(N, D), f(N, D), f(N, D), f(N, D), f(N, (CONV_B - 1) * D), f(N, D), f(N, D),
                 jax.ShapeDtypeStruct((H_B, DH_B, N), BF16), f(N, LANES), f(N, LANES), f(N, LANES), f(N, D),
                 f(N, LANES))
    (a_new, az, v, opre, zb, cb_new, uc, q, ktw, g0, wv, dn, n_new, m_new) = pl.pallas_call(
        _sample_front_body, out_shape=front_out, compiler_params=cparams, name="sample_front",
    )(x_s, st_cb.reshape(N, (CONV_B - 1) * D), st_n.reshape(N, D), m_col, m_row,
      ew['gpre'], ew['win'], ew['wv'], ew['wg'], ew['gbr'], ew['gbc'], ew['cwb'], ew['cbb'],
      ew['wq'], ew['wk'], ew['wkt'])

    blk2 = lambda w: pl.BlockSpec((bb, w), lambda i: (i, 0))
    ca_new, conv_out, c_new, numi = pl.pallas_call(
        functools.partial(_sample_state_body, N),
        grid=(NB,),
        in_specs=[pl.BlockSpec((1, bb, nst, D), lambda i: (0, i, 0, 0)), blk2(D), _const_spec(ew['cwa'].shape),
                  pl.BlockSpec((bb, H_B, DH_B, DH_B), lambda i: (i, 0, 0, 0)),
                  blk2(D), _const_spec((N, D)), _const_spec((H_B, DH_B, N)), blk2(LANES)],
        out_specs=(pl.BlockSpec((1, bb, nst, D), lambda i: (0, i, 0, 0)), blk2(D),
                   pl.BlockSpec((bb, H_B, DH_B, DH_B), lambda i: (i, 0, 0, 0)), blk2(D)),
        out_shape=(f(1, N, nst, D), f(N, D), f(N, H_B, DH_B, DH_B), f(N, D)),
        compiler_params=pltpu.CompilerParams(dimension_semantics=("arbitrary",), vmem_limit_bytes=VMEM_LIMIT),
        name="sample_state",
    )(st_ca, a_new, ew['cwa'], st_c, q, v, ktw, g0)

    ts = _rope_tables(np.full((1,), PAST_LEN))
    tabs = (ts['cq'], ts['sq'], ts['ck'], ts['sk'])
    x1, qblk, k_new, v_new, sz = pl.pallas_call(
        functools.partial(_sample_mid_body, NB),
        out_shape=(f(N, D), f(NB, H_C * bb, LANES), f(N, LANES), f(N, LANES), f(N, H_C * LANES)),
        scratch_shapes=[pltpu.VMEM((N, 2 * D), BF16)],
        compiler_params=cparams, name="sample_mid",
    )(x_s, conv_out, az, numi, v, wv, g0, dn, opre, zb, uc,
      ew['cba'], ew['lng'], ew['lnb'], ew['hng'], ew['skip'], ew['wout'], ew['gpost'],
      ow['gpre'], osw['wqs'], ow['wkv'], osw['wzs'], *tabs)

    blk3 = pl.BlockSpec((bb, WINDOW, LANES), lambda i: (i, 0, 0))
    qspec = pl.BlockSpec((1, H_C * bb, LANES), lambda i: (i, 0, 0))
    oblk, kc_new, vc_new = pl.pallas_call(
        _sample_attn_body,
        grid=(NB,),
        in_specs=[qspec, blk3, blk3, blk2(LANES), blk2(LANES), _const_spec((H_C, LANES))],
        out_specs=(qspec, blk3, blk3),
        out_shape=(f(NB, H_C * bb, LANES), f(N, WINDOW, LANES), f(N, WINDOW, LANES)),
        compiler_params=pltpu.CompilerParams(dimension_semantics=("arbitrary",), vmem_limit_bytes=VMEM_LIMIT),
        name="sample_attn",
    )(qblk, ck_win.reshape(N, WINDOW, LANES), cv_win.reshape(N, WINDOW, LANES), k_new, v_new, osw['sinkcol'])

    y_s = pl.pallas_call(
        functools.partial(_sample_back_body, NB),
        out_shape=f(N, D),
        scratch_shapes=[pltpu.VMEM((N, H_C * LANES), F32)],
        compiler_params=cparams, name="sample_back",
    )(oblk, sz, x1, osw['wos'], ow['gpost'])

    return (y_s, ca_new, cb_new.reshape(N, CONV_B - 1, D), c_new,
            n_new.reshape(N, H_B, DH_B), m_new[:, :H_B], kc_new, vc_new)


def _prep_even(j, norm_pre_e, norm_post_e, w_in_e, conv_a_w, conv_a_b, ln_a_g, ln_a_b, conv_b_w, conv_b_b,
               wq_b, wk_b, b_i, b_f, headnorm_b, skip_b, w_out_e):
    D = D_MODEL
    w_in = w_in_e[j]
    wgates = w_in[:, 7 * D:]
    gbias = jnp.concatenate([b_i[j], b_f[j]])
    row2 = lambda a: a.reshape(1, -1)
    return dict(
        gpre=row2(norm_pre_e[j]), gpost=row2(norm_post_e[j]),
        win=jnp.concatenate([w_in[:, :4 * D], w_in[:, 5 * D:7 * D]], axis=1).astype(BF16),
        wv=w_in[:, 4 * D:5 * D].astype(BF16),
        wg=jnp.pad(wgates, ((0, 0), (0, LANES - 2 * H_B))).astype(BF16),
        gbr=jnp.pad(gbias, (0, LANES - 2 * H_B)).reshape(1, LANES),
        gbc=gbias.reshape(2 * H_B, 1),
        cwa=conv_a_w[j], cba=row2(conv_a_b[j]), lng=row2(ln_a_g[j]), lnb=row2(ln_a_b[j]),
        cwb=conv_b_w[j], cbb=row2(conv_b_b[j]),
        wq=wq_b[j].astype(BF16), wqt=jnp.swapaxes(wq_b[j], 1, 2).astype(BF16),
        wk=(wk_b[j] * K_SCALE_B).astype(BF16),
        wkt=(jnp.swapaxes(wk_b[j], 1, 2) * K_SCALE_B).astype(BF16),
        hng=row2(headnorm_b[j]), skip=row2(skip_b[j]),
        wout=w_out_e[j].astype(BF16),
    )


def _prep_odd(j, norm_pre_o, norm_post_o, w_in_o, sinks, w_out_o):
    D = D_MODEL
    w_in = w_in_o[j]
    row2 = lambda a: a.reshape(1, -1)
    return dict(
        gpre=row2(norm_pre_o[j]), gpost=row2(norm_post_o[j]),
        wq=(w_in[:, :D] * Q_SCALE_C).astype(BF16),
        wkv=w_in[:, D:D + 2 * LANES].astype(BF16),
        wz=w_in[:, D + 2 * LANES:].astype(BF16),
        wo=w_out_o[j].astype(BF16),
        sinks=jnp.pad(sinks[j], (0, LANES - H_C)).reshape(1, LANES),
    )


def _prep_odd_sample(j, w_in_o, sinks, w_out_o):
    D = D_MODEL
    w_in = w_in_o[j]
    on_group = (jnp.arange(H_C)[:, None] // G_C) == jnp.arange(N_KV_C)[None, :]

    def stack_cols(w):
        w4 = w.reshape(D, H_C, 1, HD_C)
        return jnp.where(on_group[None, :, :, None], w4, 0.0).reshape(D, H_C * LANES).astype(BF16)
    wo4 = w_out_o[j].reshape(H_C, 1, HD_C, D)
    return dict(
        wqs=stack_cols(w_in[:, :D]),
        wzs=stack_cols(w_in[:, D + 2 * LANES:]),
        wos=jnp.where(on_group[:, :, None, None], wo4, 0.0).reshape(H_C * LANES, D).astype(BF16),
        sinkcol=jnp.broadcast_to(sinks[j][:, None], (H_C, LANES)),
    )


def _rope_tables(pos):
    half = ROT_DIM // 2
    f32 = np.float32
    inv = np.power(f32(ROPE_THETA), -np.arange(half, dtype=f32) * f32(2.0 / ROT_DIM)).astype(f32)
    ang = np.asarray(pos, dtype=f32)[:, None] * inv[None, :]
    cos, sin = np.cos(ang).astype(f32), np.sin(ang).astype(f32)
    n = ang.shape[0]
    pad = HD_C - ROT_DIM
    cos64 = np.concatenate([cos, cos, np.ones((n, pad), f32)], axis=1)
    sin64 = np.concatenate([-sin, sin, np.zeros((n, pad), f32)], axis=1)
    ck = np.concatenate([cos64, cos64], axis=1)
    sk = np.concatenate([sin64, sin64], axis=1)
    tabs = dict(cq=ck * f32(Q_SCALE_C), sq=sk * f32(Q_SCALE_C), ck=ck, sk=sk,
                cos_t=np.ascontiguousarray(cos.T), sin_t=np.ascontiguousarray(sin.T))
    return {k: jnp.asarray(v) for k, v in tabs.items()}


def kernel(x_prompt, x_sample, state_conv_a, state_conv_b, state_mlstm_c, state_mlstm_n, state_mlstm_m, cache_k_win, cache_v_win, norm_pre_e, norm_post_e, w_in_e, conv_a_w, conv_a_b, ln_a_g, ln_a_b, conv_b_w, conv_b_b, wq_b, wk_b, b_i, b_f, headnorm_b, skip_b, w_out_e, norm_pre_o, norm_post_o, w_in_o, sinks, w_out_o):
    B, L, D = x_prompt.shape
    ew = _prep_even(0, norm_pre_e, norm_post_e, w_in_e, conv_a_w, conv_a_b, ln_a_g, ln_a_b, conv_b_w,
                    conv_b_b, wq_b, wk_b, b_i, b_f, headnorm_b, skip_b, w_out_e)
    ow = _prep_odd(0, norm_pre_o, norm_post_o, w_in_o, sinks, w_out_o)

    x1, ca_p, cb_p, c_p, n_p, m_p = _even_prompt(x_prompt, ew, min(TILE_EVEN, L))
    tp = _rope_tables(np.arange(L))
    y_p, kw_p, vw_p = _odd_prompt(x1, ow, (tp['cos_t'], tp['sin_t'], tp['ck'], tp['sk']), min(TILE_ODD, L))

    conv_a_p = ca_p[None, :, 32 - (CONV_A - 1):, :]
    conv_b_p = cb_p[None, :, 8 - (CONV_B - 1):, :]
    mlstm_c_p = c_p[None]
    mlstm_n_p = n_p[None]
    mlstm_m_p = m_p[None, :, :H_B, 0]
    k_win_p = kw_p.reshape(1, B, WINDOW, N_KV_C, HD_C)
    v_win_p = vw_p.reshape(1, B, WINDOW, N_KV_C, HD_C)

    N = x_sample.shape[0]
    osw = _prep_odd_sample(0, w_in_o, sinks, w_out_o)
    y_s, ca_s, cb_s, c_s, n_s, m_s, kw_s, vw_s = _sample_path(
        x_sample.reshape(N, D), state_conv_a, state_conv_b[0], state_mlstm_c[0], state_mlstm_n[0],
        state_mlstm_m[0], cache_k_win[0], cache_v_win[0], ew, ow, osw)
    return (y_p, y_s.reshape(N, 1, D), conv_a_p, conv_b_p, mlstm_c_p, mlstm_n_p, mlstm_m_p, k_win_p, v_win_p,
            ca_s, cb_s[None], c_s[None], n_s[None], m_s[None],
            kw_s.reshape(1, N, WINDOW, N_KV_C, HD_C), vw_s.reshape(1, N, WINDOW, N_KV_C, HD_C))
```

```python
import functools

import jax
import jax.numpy as jnp
import numpy as np
from jax import lax
from jax.experimental import pallas as pl
from jax.experimental.pallas import tpu as pltpu

F32 = jnp.float32
BF16 = jnp.bfloat16

D_MODEL = 1024
EPS = 1e-6
PAST_LEN = 16384
CONV_A = 31
CONV_B = 4
H_B = 4
DH_B = D_MODEL // H_B
HD_C = 64
H_C = D_MODEL // HD_C
N_KV_C = 2
G_C = H_C // N_KV_C
WINDOW = 128
ROT_DIM = HD_C // 4
ROPE_THETA = 500000.0
K_SCALE_B = DH_B ** -0.5
Q_SCALE_C = HD_C ** -0.5

LANES = 128
NG = D_MODEL // LANES
STRIDE = 4
TILE_EVEN = 256
TILE_ODD = 512
ROWS = 16
AUG = 16
SAMPLE_BLOCK = 8
VMEM_LIMIT = 56 * 1024 * 1024

NT_DIMS = (((1,), (1,)), ((), ()))
TN_DIMS = (((0,), (0,)), ((), ()))


def _dot(a, b):
    return jnp.dot(a, b, preferred_element_type=F32)


def _dot_nt(a, b):
    return lax.dot_general(a, b, NT_DIMS, preferred_element_type=F32)


def _sigmoid(x):
    return 1.0 / (1.0 + jnp.exp(-x))


def _silu(x):
    return x * _sigmoid(x)


def _log_sigmoid(x):
    return jnp.minimum(x, 0.0) - jnp.log1p(jnp.exp(-jnp.abs(x)))


def _rms(x, g):
    ms = jnp.mean(x * x, axis=-1, keepdims=True)
    return x * lax.rsqrt(ms + EPS) * g


def _layernorm(x, g, b):
    mu = jnp.mean(x, axis=-1, keepdims=True)
    xc = x - mu
    var = jnp.mean(xc * xc, axis=-1, keepdims=True)
    return xc * lax.rsqrt(var + EPS) * g + b


def _row_loop(total, rows, body):
    for i in range(total // rows):
        body(i * rows)


def _split3(a):
    hi = a.astype(BF16)
    r1 = a - hi.astype(F32)
    mid = r1.astype(BF16)
    lo = (r1 - mid.astype(F32)).astype(BF16)
    return hi, mid, lo


def _conv_strided(src, w_ref, dst, taps, off, total):
    chunk_rows = 8 * STRIDE

    for c in range(total // chunk_rows):
        base = c * chunk_rows
        for g in range(NG):
            gs = slice(g * LANES, (g + 1) * LANES)
            accs = [None] * STRIDE
            rows = [src[g, pl.ds(base + off + s, 8, stride=STRIDE), :] for s in range(taps + STRIDE - 1)]
            for j in range(taps):
                wj = jnp.broadcast_to(w_ref[j:j + 1, gs], (8, LANES))
                for p in range(STRIDE):
                    v = rows[j + p] * wj
                    accs[p] = v if j == 0 else accs[p] + v
            for p in range(STRIDE):
                dst[g, pl.ds(base + p, 8, stride=STRIDE), :] = accs[p]


def _even_prompt_body(T, NT, x_ref, gpre_ref, gpost_ref, win_ref, wvt_ref, wg_ref, gbr_ref,
                      cwa_ref, cba_ref, lng_ref, lnb_ref, cwb_ref, cbb_ref, wqt_ref, wk_ref,
                      hng_ref, skip_ref, wout_ref,
                      y_ref, ca_ref, cb_ref, c_ref, n_ref, m_ref,
                      hn_s, p_s, abuf, ubuf, cbuf, uc_s, ucb_s, ycat_s, cst_s, mst_s,
                      st_s, st2_s, rinv_s, st4_s, rinv4_s, gz_s, vt_s):
    D = D_MODEL
    R = ROWS
    t = pl.program_id(1)

    @pl.when(t == 0)
    def _():
        abuf[:, 0:32, :] = jnp.zeros((NG, 32, LANES), F32)
        ubuf[:, 0:8, :] = jnp.zeros((NG, 8, LANES), F32)
        cst_s[...] = jnp.zeros(cst_s.shape, F32)
        mst_s[...] = jnp.zeros(mst_s.shape, F32)

    _rms_scale(T, lambda sl, gs: x_ref[0, sl, gs], st_s, rinv_s, D)

    def p_norm(r0):
        sl = pl.ds(r0, R)
        rinv = rinv_s[sl, :]
        for g in range(NG):
            gs = slice(g * LANES, (g + 1) * LANES)
            hn_s[sl, gs] = (x_ref[0, sl, gs] * rinv * gpre_ref[:, gs]).astype(BF16)
    _row_loop(T, R, p_norm)

    hn = hn_s[...]

    def proj(p):
        return _dot(hn, win_ref[:, p * D:(p + 1) * D])
    p_s[0] = proj(0)
    p_s[1] = proj(1)
    gz_s[0] = proj(2).astype(BF16)
    u_new = proj(3)
    for g in range(NG):
        ubuf[g, 8:8 + T, :] = u_new[:, g * LANES:(g + 1) * LANES]
    vt_s[...] = _dot_nt(wvt_ref[...], hn).astype(BF16)
    p_s[2] = proj(4)
    gz_s[1] = proj(5).astype(BF16)
    gc = _dot(hn, wg_ref[...]) + gbr_ref[...]

    row = lax.broadcasted_iota(jnp.int32, (T, T), 0)
    col = lax.broadcasted_iota(jnp.int32, (T, T), 1)
    tril = jnp.where(row >= col, 1.0, 0.0).astype(BF16)
    keep = row <= col
    ch, cm, cl = _split3(_log_sigmoid(gc))
    b_c = _dot(tril, ch) + _dot(tril, cm) + _dot(tril, cl)
    b_t = b_c.T

    def p_glu(r0):
        sl = pl.ds(r0, R)
        for g in range(NG):
            gs = slice(g * LANES, (g + 1) * LANES)
            abuf[g, pl.ds(r0 + 32, R), :] = p_s[0, sl, gs] * _sigmoid(p_s[1, sl, gs])
    _row_loop(T, R, p_glu)

    _conv_strided(abuf, cwa_ref, cbuf, CONV_A, 32 - (CONV_A - 1), T)

    ones = jnp.ones((LANES, LANES), BF16)

    def p_ln_mean(r0):
        sl = pl.ds(r0, R)
        acc = None
        for g in range(NG):
            x = cbuf[g, sl, :] + cba_ref[:, g * LANES:(g + 1) * LANES]
            cbuf[g, sl, :] = x
            acc = x if acc is None else acc + x
        hi = acc.astype(BF16)
        st_s[sl, :] = hi
        st2_s[sl, :] = (acc - hi.astype(F32)).astype(BF16)
    _row_loop(T, R, p_ln_mean)
    rinv_s[...] = (_dot(st_s[...], ones) + _dot(st2_s[...], ones)) * (1.0 / D)

    def p_ln_center(r0):
        sl = pl.ds(r0, R)
        mu = rinv_s[sl, :]
        acc = None
        for g in range(NG):
            xc = cbuf[g, sl, :] - mu
            cbuf[g, sl, :] = xc
            acc = xc * xc if acc is None else acc + xc * xc
        st_s[sl, :] = acc.astype(BF16)
    _row_loop(T, R, p_ln_center)
    rinv_s[...] = lax.rsqrt(_dot(st_s[...], ones) * (1.0 / D) + EPS)

    def p_ln(r0):
        sl = pl.ds(r0, R)
        rs = rinv_s[sl, :]
        for g in range(NG):
            gs = slice(g * LANES, (g + 1) * LANES)
            yv = cbuf[g, sl, :] * rs * lng_ref[:, gs] + lnb_ref[:, gs]
            ycat_s[sl, gs] = _silu(yv.astype(BF16)) * _silu(gz_s[0, sl, gs])
    _row_loop(T, R, p_ln)

    _conv_strided(ubuf, cwb_ref, cbuf, CONV_B, 8 - (CONV_B - 1), T)

    def p_uc(r0):
        sl = pl.ds(r0, R)
        for g in range(NG):
            gs = slice(g * LANES, (g + 1) * LANES)
            uc = _silu(cbuf[g, sl, :] + cbb_ref[:, gs])
            uc_s[sl, gs] = uc
            ucb_s[sl, gs] = uc.astype(BF16)
    _row_loop(T, R, p_uc)

    ones_rows = jnp.ones((AUG, T), BF16)
    for h in range(H_B):
        hs = slice(h * DH_B, (h + 1) * DH_B)
        ub = ucb_s[:, hs]
        kb = _dot(ub, wk_ref[h])
        qt = _dot_nt(wqt_ref[h], ub).astype(BF16)
        st = _dot(kb.astype(BF16), qt)
        u_c = gc[:, h:h + 1] - b_c[:, H_B + h:H_B + h + 1]
        b_r = b_t[H_B + h:H_B + h + 1, :]
        m_h = mst_s[h:h + 1, 0:1]
        dm = jnp.where(keep, b_r + u_c, -jnp.inf)
        inter = b_r + m_h
        mt = jnp.maximum(inter, jnp.max(dm, axis=0, keepdims=True))
        wt = (st * jnp.exp(dm - mt)).astype(BF16)
        a_in = jnp.exp(inter - mt)
        vta = jnp.concatenate([vt_s[hs, :], ones_rows], axis=0)
        cta = cst_s[h]
        numt = _dot(vta, wt) + a_in * _dot(cta.astype(BF16), qt)
        den = numt[DH_B:DH_B + 1, :]
        ht = numt[0:DH_B, :] * (1.0 / jnp.maximum(jnp.abs(den), jnp.exp(-mt)))
        p_s[0, :, hs] = ht.T
        bl = b_r[:, T - 1:T]
        wl = bl + u_c
        m_new = jnp.maximum(bl + m_h, jnp.max(wl, axis=0, keepdims=True))
        g0 = jnp.exp(bl + m_h - m_new)
        kw = (kb * jnp.exp(wl - m_new)).astype(BF16)
        cst_s[h] = g0 * cta + _dot(vta, kw)
        mst_s[h:h + 1, :] = jnp.broadcast_to(m_new, (1, LANES))

    gph = DH_B // LANES

    def p_ogate(r0):
        sl = pl.ds(r0, R)
        for h in range(H_B):
            acc = None
            for gg in range(gph):
                gs = slice((h * gph + gg) * LANES, (h * gph + gg + 1) * LANES)
                o = _sigmoid(p_s[2, sl, gs]) * p_s[0, sl, gs]
                p_s[0, sl, gs] = o
                acc = o * o if acc is None else acc + o * o
            st4_s[sl, h * LANES:(h + 1) * LANES] = acc.astype(BF16)
    _row_loop(T, R, p_ogate)
    for h in range(H_B):
        hl = slice(h * LANES, (h + 1) * LANES)
        rinv4_s[:, hl] = lax.rsqrt(_dot(st4_s[:, hl], ones) * (1.0 / DH_B) + EPS)

    def p_hb(r0):
        sl = pl.ds(r0, R)
        for g in range(NG):
            gs = slice(g * LANES, (g + 1) * LANES)
            h = g // gph
            hb = p_s[0, sl, gs] * rinv4_s[sl, h * LANES:(h + 1) * LANES] * hng_ref[:, gs]
            hb = (hb + skip_ref[:, gs] * uc_s[sl, gs]).astype(BF16) * _silu(gz_s[1, sl, gs])
            ycat_s[sl, D + g * LANES:D + (g + 1) * LANES] = hb
    _row_loop(T, R, p_hb)

    p_s[1] = _dot(ycat_s[...], wout_ref[...])
    _rms_scale(T, lambda sl, gs: p_s[1, sl, gs], st_s, rinv_s, D)

    def p_out(r0):
        sl = pl.ds(r0, R)
        rinv = rinv_s[sl, :]
        for g in range(NG):
            gs = slice(g * LANES, (g + 1) * LANES)
            y_ref[0, sl, gs] = x_ref[0, sl, gs] + p_s[1, sl, gs] * rinv * gpost_ref[:, gs]
    _row_loop(T, R, p_out)

    @pl.when(t == NT - 1)
    def _():
        for g in range(NG):
            ca_ref[0, :, g * LANES:(g + 1) * LANES] = abuf[g, T:T + 32, :]
            cb_ref[0, :, g * LANES:(g + 1) * LANES] = ubuf[g, T:T + 8, :]
        for h in range(H_B):
            c_ref[0, h] = cst_s[h, 0:DH_B, :].T
            n_ref[0, h:h + 1, :] = cst_s[h, DH_B:DH_B + 1, :]
        m_ref[0] = mst_s[...]

    abuf[:, 0:32, :] = abuf[:, T:T + 32, :]
    ubuf[:, 0:8, :] = ubuf[:, T:T + 8, :]


def _const_spec(shape):
    nd = len(shape)
    return pl.BlockSpec(shape, lambda *_: (0,) * nd, pipeline_mode=pl.Buffered(1))


def _even_prompt(x, ew, T):
    B, L, D = x.shape
    NT = L // T
    consts = (ew['gpre'], ew['gpost'], ew['win'], ew['wvt'], ew['wg'], ew['gbr'],
              ew['cwa'], ew['cba'], ew['lng'], ew['lnb'], ew['cwb'], ew['cbb'], ew['wqt'], ew['wk'],
              ew['hng'], ew['skip'], ew['wout'])
    in_specs = [pl.BlockSpec((1, T, D), lambda b, t: (b, t, 0))] + [_const_spec(c.shape) for c in consts]
    out_shape = (
        jax.ShapeDtypeStruct((B, L, D), F32),
        jax.ShapeDtypeStruct((B, 32, D), F32),
        jax.ShapeDtypeStruct((B, 8, D), F32),
        jax.ShapeDtypeStruct((B, H_B, DH_B, DH_B), F32),
        jax.ShapeDtypeStruct((B, H_B, DH_B), F32),
        jax.ShapeDtypeStruct((B, 8, LANES), F32),
    )
    out_specs = (
        pl.BlockSpec((1, T, D), lambda b, t: (b, t, 0)),
        pl.BlockSpec((1, 32, D), lambda b, t: (b, 0, 0)),
        pl.BlockSpec((1, 8, D), lambda b, t: (b, 0, 0)),
        pl.BlockSpec((1, H_B, DH_B, DH_B), lambda b, t: (b, 0, 0, 0)),
        pl.BlockSpec((1, H_B, DH_B), lambda b, t: (b, 0, 0)),
        pl.BlockSpec((1, 8, LANES), lambda b, t: (b, 0, 0)),
    )
    scratch = [
        pltpu.VMEM((T, D), BF16),
        pltpu.VMEM((3, T, D), F32),
        pltpu.VMEM((NG, T + 32, LANES), F32),
        pltpu.VMEM((NG, T + 8, LANES), F32),
        pltpu.VMEM((NG, T, LANES), F32),
        pltpu.VMEM((T, D), F32),
        pltpu.VMEM((T, D), BF16),
        pltpu.VMEM((T, 2 * D), BF16),
        pltpu.VMEM((H_B, DH_B + AUG, DH_B), F32),
        pltpu.VMEM((8, LANES), F32),
        pltpu.VMEM((T, LANES), BF16),
        pltpu.VMEM((T, LANES), BF16),
        pltpu.VMEM((T, LANES), F32),
        pltpu.VMEM((T, H_B * LANES), BF16),
        pltpu.VMEM((T, H_B * LANES), F32),
        pltpu.VMEM((2, T, D), BF16),
        pltpu.VMEM((D, T), BF16),
    ]
    return pl.pallas_call(
        functools.partial(_even_prompt_body, T, NT),
        grid=(B, NT),
        in_specs=in_specs,
        out_specs=out_specs,
        out_shape=out_shape,
        scratch_shapes=scratch,
        compiler_params=pltpu.CompilerParams(
            dimension_semantics=("arbitrary", "arbitrary"), vmem_limit_bytes=VMEM_LIMIT),
        name="even_prompt",
    )(x, *consts)


def _rope_slab(x, cos, sin, lane_lo):
    partner = jnp.where(lane_lo, pltpu.roll(x, LANES - ROT_DIM // 2, 1), pltpu.roll(x, ROT_DIM // 2, 1))
    return x * cos + partner * sin


def _rms_scale(T, src, st_s, rinv_s, width):
    R = ROWS
    ng = width // LANES

    def p_sq(r0):
        sl = pl.ds(r0, R)
        acc = None
        for g in range(ng):
            xg = src(sl, slice(g * LANES, (g + 1) * LANES))
            acc = xg * xg if acc is None else acc + xg * xg
        st_s[sl, :] = acc.astype(BF16)
    _row_loop(T, R, p_sq)
    ms = _dot(st_s[...], jnp.ones((LANES, LANES), BF16))
    rinv_s[...] = lax.rsqrt(ms * (1.0 / width) + EPS)


def _odd_prompt_body(T, NT, x_ref, gpre_ref, gpost_ref, wqt_ref, wkv_ref, wz_ref, wo_ref, sinks_ref,
                     cqt_ref, sqt_ref, ck_ref, sk_ref,
                     y_ref, kw_ref, vw_ref,
                     hn_s, st_s, rinv_s, py_s, pz_s, pkv_s, qt_s, kr_s, kb_s, vf_s, o_s, g_s):
    D = D_MODEL
    R = ROWS
    W = WINDOW
    t = pl.program_id(1)

    @pl.when(t == 0)
    def _():
        kb_s[0:W, :] = jnp.zeros((W, LANES), BF16)
        vf_s[:, 0:W, :] = jnp.zeros((4, W, LANES), BF16)

    _rms_scale(T, lambda sl, gs: x_ref[0, sl, gs], st_s, rinv_s, D)

    def p_norm(r0):
        sl = pl.ds(r0, R)
        rinv = rinv_s[sl, :]
        for g in range(NG):
            gs = slice(g * LANES, (g + 1) * LANES)
            hn_s[sl, gs] = (x_ref[0, sl, gs] * rinv * gpre_ref[:, gs]).astype(BF16)
    _row_loop(T, R, p_norm)

    hn = hn_s[...]
    qt = _dot_nt(wqt_ref[...], hn)
    pkv_s[...] = _dot(hn, wkv_ref[...])
    pz_s[...] = _dot(hn, wz_ref[...]).astype(BF16)

    cq, sq = cqt_ref[...], sqt_ref[...]
    half = ROT_DIM // 2
    for h in range(H_C):
        b0 = h * HD_C
        x1 = qt[b0:b0 + half, :]
        x2 = qt[b0 + half:b0 + ROT_DIM, :]
        rot = jnp.concatenate([x1 * cq - x2 * sq, x2 * cq + x1 * sq, qt[b0 + ROT_DIM:b0 + HD_C, :]], axis=0)
        qt_s[b0:b0 + HD_C, :] = rot.astype(BF16)

    RK = 64
    for c in range(T // RK):
        sl = slice(c * RK, (c + 1) * RK)
        lane = lax.broadcasted_iota(jnp.int32, (RK, LANES), 1)
        lane_lo = (lane % HD_C) < (ROT_DIM // 2)
        head_lo = lane < HD_C
        kr = _rope_slab(pkv_s[sl, 0:LANES], ck_ref[sl, :], sk_ref[sl, :], lane_lo)
        kr_s[sl, :] = kr
        vr = pkv_s[sl, LANES:2 * LANES]
        vrr = pltpu.roll(vr, HD_C, 1)
        dst = slice(W + c * RK, W + (c + 1) * RK)
        zero = jnp.zeros_like(kr)
        kb_s[dst, :] = kr.astype(BF16)
        vf_s[0, dst, :] = jnp.where(head_lo, vr, zero).astype(BF16)
        vf_s[1, dst, :] = jnp.where(head_lo, zero, vrr).astype(BF16)
        vf_s[2, dst, :] = jnp.where(head_lo, vrr, zero).astype(BF16)
        vf_s[3, dst, :] = jnp.where(head_lo, zero, vr).astype(BF16)

    kj = lax.broadcasted_iota(jnp.int32, (2 * W, W), 0)
    qi = lax.broadcasted_iota(jnp.int32, (2 * W, W), 1)
    band = (kj >= qi) & (kj <= qi + W)
    for i in range(T // W):
        rows = slice(i * W, (i + 1) * W)
        krows = slice(i * W, i * W + 2 * W)
        if i == 0:
            valid = band & ((kj >= W) | (t > 0))
        else:
            valid = band
        bias = jnp.where(valid, 0.0, -jnp.inf)
        bias2 = jnp.concatenate([bias, bias], axis=1)
        first_head = lax.broadcasted_iota(jnp.int32, (1, 2 * W), 1) < W
        kk = kb_s[krows, :]
        zpad = jnp.zeros((HD_C, 2 * W), BF16)
        for g in range(N_KV_C):
            vv2 = jnp.concatenate([vf_s[2 * g, krows, :], vf_s[2 * g + 1, krows, :]], axis=0)
            for pp in range(G_C // 2):
                pair = g * (G_C // 2) + pp
                ls = slice(pair * LANES, (pair + 1) * LANES)
                qq = jnp.concatenate([qt_s[pair * LANES:pair * LANES + HD_C, rows],
                                      qt_s[pair * LANES + HD_C:(pair + 1) * LANES, rows]], axis=1)
                rhs = jnp.concatenate([qq, zpad] if g == 0 else [zpad, qq], axis=0)
                sh = _dot(kk, rhs) + bias2
                sink = jnp.where(first_head, sinks_ref[0:1, 2 * pair:2 * pair + 1],
                                 sinks_ref[0:1, 2 * pair + 1:2 * pair + 2])
                m = jnp.maximum(jnp.max(sh, axis=0, keepdims=True), sink)
                pe = jnp.exp(sh - m)
                den = jnp.sum(pe, axis=0, keepdims=True) + jnp.exp(sink - m)
                p = (pe * (1.0 / den)).astype(BF16)
                p2t = jnp.concatenate([p[:, 0:W], p[:, W:2 * W]], axis=0)
                o_s[rows, ls] = lax.dot_general(p2t, vv2, TN_DIMS, preferred_element_type=F32)

    def p_gate(r0):
        sl = pl.ds(r0, R)
        g_s[sl, :] = o_s[sl, :].astype(BF16) * _silu(pz_s[sl, :])
    _row_loop(T, R, p_gate)

    py_s[...] = _dot(g_s[...], wo_ref[...])
    _rms_scale(T, lambda sl, gs: py_s[sl, gs], st_s, rinv_s, D)

    def p_out(r0):
        sl = pl.ds(r0, R)
        rinv = rinv_s[sl, :]
        for g in range(NG):
            gs = slice(g * LANES, (g + 1) * LANES)
            y_ref[0, sl, gs] = x_ref[0, sl, gs] + py_s[sl, gs] * rinv * gpost_ref[:, gs]
    _row_loop(T, R, p_out)

    @pl.when(t == NT - 1)
    def _():
        kw_ref[0] = kr_s[T - W:T, :]
        vw_ref[0] = pkv_s[T - W:T, LANES:2 * LANES]

    kb_s[0:W, :] = kb_s[T:T + W, :]
    vf_s[:, 0:W, :] = vf_s[:, T:T + W, :]


def _odd_prompt(x, ow, tabs, T):
    B, L, D = x.shape
    NT = L // T
    consts = (ow['gpre'], ow['gpost'], ow['wqt'], ow['wkv'], ow['wz'], ow['wo'], ow['sinks'])
    tab_spec = pl.BlockSpec((T, LANES), lambda b, t: (t, 0))
    tabt_spec = pl.BlockSpec((ROT_DIM // 2, T), lambda b, t: (0, t))
    in_specs = ([pl.BlockSpec((1, T, D), lambda b, t: (b, t, 0))] + [_const_spec(c.shape) for c in consts]
                + [tabt_spec] * 2 + [tab_spec] * 2)
    out_shape = (
        jax.ShapeDtypeStruct((B, L, D), F32),
        jax.ShapeDtypeStruct((B, WINDOW, LANES), F32),
        jax.ShapeDtypeStruct((B, WINDOW, LANES), F32),
    )
    out_specs = (
        pl.BlockSpec((1, T, D), lambda b, t: (b, t, 0)),
        pl.BlockSpec((1, WINDOW, LANES), lambda b, t: (b, 0, 0)),
        pl.BlockSpec((1, WINDOW, LANES), lambda b, t: (b, 0, 0)),
    )
    scratch = [
        pltpu.VMEM((T, D), BF16),
        pltpu.VMEM((T, LANES), BF16),
        pltpu.VMEM((T, LANES), F32),
        pltpu.VMEM((T, D), F32),
        pltpu.VMEM((T, D), BF16),
        pltpu.VMEM((T, 2 * LANES), F32),
        pltpu.VMEM((D, T), BF16),
        pltpu.VMEM((T, LANES), F32),
        pltpu.VMEM((T + WINDOW, LANES), BF16),
        pltpu.VMEM((4, T + WINDOW, LANES), BF16),
        pltpu.VMEM((T, D), F32),
        pltpu.VMEM((T, D), BF16),
    ]
    return pl.pallas_call(
        functools.partial(_odd_prompt_body, T, NT),
        grid=(B, NT),
        in_specs=in_specs,
        out_specs=out_specs,
        out_shape=out_shape,
        scratch_shapes=scratch,
        compiler_params=pltpu.CompilerParams(
            dimension_semantics=("arbitrary", "arbitrary"), vmem_limit_bytes=VMEM_LIMIT),
        name="odd_prompt",
    )(x, *consts, *tabs)


def _sample_front_body(x_ref, cbs_ref, n_ref, mc_ref, mr_ref, gpre_ref, win_ref, wvp_ref, wg_ref, wgt_ref, gbr_ref,
                       gbc_ref, cwb_ref, cbb_ref, wq_ref, wk_ref, wkt_ref,
                       anew_ref, az_ref, v_ref, op_ref, zb_ref, cbo_ref, uc_ref, q_ref, ktw_ref,
                       g0_ref, wv_ref, dn_ref, nn_ref, mn_ref):
    D = D_MODEL
    hn = _rms(x_ref[...], gpre_ref[...]).astype(BF16)

    def proj(p):
        return _dot(hn, win_ref[:, p * D:(p + 1) * D])
    anew_ref[...] = proj(0) * _sigmoid(proj(1))
    az_ref[...] = proj(2)
    u = proj(3)
    v_ref[...] = _dot(hn, wvp_ref[...])
    op_ref[...] = proj(4)
    zb_ref[...] = proj(5)
    gc = _dot(hn, wg_ref[...]) + gbr_ref[...]
    gr = _dot_nt(wgt_ref[...], hn) + gbc_ref[...]

    acc = u * cwb_ref[CONV_B - 1:CONV_B, :]
    for j in range(CONV_B - 1):
        acc = acc + cbs_ref[:, j * D:(j + 1) * D] * cwb_ref[j:j + 1, :]
    uc = _silu(acc + cbb_ref[...])
    uc_ref[...] = uc
    cbo_ref[:, 0:(CONV_B - 2) * D] = cbs_ref[:, D:(CONV_B - 1) * D]
    cbo_ref[:, (CONV_B - 2) * D:] = u

    ucb = uc.astype(BF16)
    g0_ref[...] = jnp.zeros(g0_ref.shape, F32)
    wv_ref[...] = jnp.zeros(wv_ref.shape, F32)
    dn_ref[...] = jnp.ones(dn_ref.shape, F32)
    mn_ref[...] = jnp.zeros(mn_ref.shape, F32)
    for h in range(H_B):
        hs = slice(h * DH_B, (h + 1) * DH_B)
        ub = ucb[:, hs]
        q = _dot(ub, wq_ref[h])
        k = _dot(ub, wk_ref[h])
        kt = _dot_nt(wkt_ref[h], ub)
        q_ref[:, hs] = q
        nrow = n_ref[:, hs]
        li_c = gc[:, h:h + 1]
        lf_c = _log_sigmoid(gc[:, H_B + h:H_B + h + 1])
        m_c = mc_ref[:, h:h + 1]
        m_new = jnp.maximum(lf_c + m_c, li_c)
        g0 = jnp.exp(lf_c + m_c - m_new)
        ws = jnp.exp(li_c - m_new)
        w = jnp.sum(q * k, axis=1, keepdims=True) * ws
        den = w + g0 * jnp.sum(q * nrow, axis=1, keepdims=True)
        g0_ref[:, h:h + 1] = g0
        wv_ref[:, h:h + 1] = w
        dn_ref[:, h:h + 1] = jnp.maximum(jnp.abs(den), jnp.exp(-m_new))
        mn_ref[:, h:h + 1] = m_new
        nn_ref[:, hs] = g0 * nrow + ws * k
        li_r = gr[h:h + 1, :]
        lf_r = _log_sigmoid(gr[H_B + h:H_B + h + 1, :])
        m_r = mr_ref[h:h + 1, :]
        ws_r = jnp.exp(li_r - jnp.maximum(lf_r + m_r, li_r))
        ktw_ref[h] = (kt * ws_r).astype(BF16)


def _sample_state_body(N, ca_ref, anew_ref, cwa_ref, c_ref, q_ref, v_ref, ktw_ref, g0_ref,
                       cao_ref, co_ref, cn_ref, numi_ref):
    D = D_MODEL
    bb = SAMPLE_BLOCK
    i = pl.program_id(0)
    nst = CONV_A - 1
    for j in range(bb):
        a_new = anew_ref[j:j + 1, :]
        past = jnp.sum(ca_ref[0, j] * cwa_ref[0:nst, :], axis=0, keepdims=True)
        co_ref[j:j + 1, :] = past + a_new * cwa_ref[nst:nst + 1, :]
        cao_ref[0, j, 0:nst - 1, :] = ca_ref[0, j, 1:nst, :]
        cao_ref[0, j, nst - 1:nst, :] = a_new

    rown = lax.broadcasted_iota(jnp.int32, (N, DH_B), 0)
    rowb = lax.broadcasted_iota(jnp.int32, (bb, DH_B), 0)
    for h in range(H_B):
        hs = slice(h * DH_B, (h + 1) * DH_B)
        qh = q_ref[:, hs].astype(BF16)
        vh = v_ref[:, hs]
        ktw = ktw_ref[h]
        numi = jnp.zeros((bb, DH_B), F32)
        for j in range(bb):
            cm = c_ref[j, h]
            r = _dot(qh, cm.astype(BF16))
            numi = jnp.where(rowb == j, r, numi)
            vsel = jnp.where(rown == i * bb + j, vh, 0.0).astype(BF16)
            cn_ref[j, h] = g0_ref[j:j + 1, h:h + 1] * cm + _dot(ktw, vsel)
        numi_ref[:, hs] = numi


def _sample_mid_body(NB, x_ref, co_ref, az_ref, numi_ref, v_ref, wv_ref, g0_ref, dn_ref, op_ref, zb_ref, uc_ref,
                     cba_ref, lng_ref, lnb_ref, hng_ref, skip_ref, wout_ref, gpost_ref,
                     gpre_ref, wqs_ref, wkv_ref, wzs_ref, cq_ref, sq_ref, ck_ref, sk_ref,
                     x1_ref, qblk_ref, kn_ref, vn_ref, sz_ref, ycat_s):
    D = D_MODEL
    bb = SAMPLE_BLOCK
    ya = _silu(_layernorm(co_ref[...] + cba_ref[...], lng_ref[...], lnb_ref[...])) * _silu(az_ref[...])
    ycat_s[:, 0:D] = ya.astype(BF16)
    for h in range(H_B):
        hs = slice(h * DH_B, (h + 1) * DH_B)
        num = wv_ref[:, h:h + 1] * v_ref[:, hs] + g0_ref[:, h:h + 1] * numi_ref[:, hs]
        o = _sigmoid(op_ref[:, hs]) * (num / dn_ref[:, h:h + 1])
        hb = _rms(o, hng_ref[:, hs])
        hb = (hb + skip_ref[:, hs] * uc_ref[:, hs]) * _silu(zb_ref[:, hs])
        ycat_s[:, D + h * DH_B:D + (h + 1) * DH_B] = hb.astype(BF16)
    x1 = x_ref[...] + _rms(_dot(ycat_s[...], wout_ref[...]), gpost_ref[...])
    x1_ref[...] = x1

    hn = _rms(x1, gpre_ref[...]).astype(BF16)
    lane = lax.broadcasted_iota(jnp.int32, (x1.shape[0], LANES), 1)
    lane_lo = (lane % HD_C) < (ROT_DIM // 2)
    kv = _dot(hn, wkv_ref[...])
    kn_ref[...] = _rope_slab(kv[:, 0:LANES], ck_ref[...], sk_ref[...], lane_lo)
    vn_ref[...] = kv[:, LANES:2 * LANES]
    sz_ref[...] = _silu(_dot(hn, wzs_ref[...]))
    qs = _dot(hn, wqs_ref[...])
    for h in range(H_C):
        qh = _rope_slab(qs[:, h * LANES:(h + 1) * LANES], cq_ref[...], sq_ref[...], lane_lo)
        for blk in range(NB):
            qblk_ref[blk, h * bb:(h + 1) * bb, :] = qh[blk * bb:(blk + 1) * bb, :]


def _sample_attn_body(q_ref, kc_ref, vc_ref, kn_ref, vn_ref, sink_ref, o_ref, kco_ref, vco_ref):
    bb = SAMPLE_BLOCK
    W = WINDOW
    qs = [q_ref[0, pl.ds(j, H_C, stride=bb), :] for j in range(bb)]
    s = jnp.concatenate([_dot_nt(qs[j].astype(BF16), kc_ref[j].astype(BF16)) for j in range(bb)], axis=0)
    sn = jnp.concatenate([jnp.sum(qs[j] * kn_ref[j:j + 1, :], axis=1, keepdims=True) for j in range(bb)], axis=0)
    sink = jnp.concatenate([sink_ref[:, 0:1]] * bb, axis=0)
    m = jnp.maximum(jnp.maximum(jnp.max(s, axis=1, keepdims=True), sn), sink)
    p = jnp.exp(s - m)
    pn = jnp.exp(sn - m)
    rden = 1.0 / (jnp.sum(p, axis=1, keepdims=True) + pn + jnp.exp(sink - m))
    pb = (p * rden).astype(BF16)
    pn = pn * rden
    for j in range(bb):
        rows = slice(j * H_C, (j + 1) * H_C)
        kn = kn_ref[j:j + 1, :]
        vn = vn_ref[j:j + 1, :]
        o_ref[0, pl.ds(j, H_C, stride=bb), :] = _dot(pb[rows, :], vc_ref[j].astype(BF16)) + pn[rows, :] * vn
        kco_ref[j, 0:W - 1, :] = kc_ref[j, 1:W, :]
        kco_ref[j, W - 1:W, :] = kn
        vco_ref[j, 0:W - 1, :] = vc_ref[j, 1:W, :]
        vco_ref[j, W - 1:W, :] = vn


def _sample_back_body(NB, o_ref, sz_ref, x1_ref, wos_ref, gpost_ref, y_ref, g_s):
    bb = SAMPLE_BLOCK
    for blk in range(NB):
        rs = slice(blk * bb, (blk + 1) * bb)
        for h in range(H_C):
            ls = slice(h * LANES, (h + 1) * LANES)
            g_s[rs, ls] = o_ref[blk, h * bb:(h + 1) * bb, :] * sz_ref[rs, ls]
    y = _dot(g_s[...].astype(BF16), wos_ref[...])
    y_ref[...] = x1_ref[...] + _rms(y, gpost_ref[...])


def _sample_path(x_s, st_ca, st_cb, st_c, st_n, st_m, ck_win, cv_win, ew, ow, osw):
    N, D = x_s.shape
    bb = SAMPLE_BLOCK
    NB = N // bb
    nst = CONV_A - 1
    f = lambda *shape: jax.ShapeDtypeStruct(shape, F32)
    cparams = pltpu.CompilerParams(vmem_limit_bytes=VMEM_LIMIT)

    m_col = jnp.pad(st_m, ((0, 0), (0, LANES - H_B)))
    m_row = jnp.pad(st_m.T, ((0, 8 - H_B), (0, 0)))
    front_out = (f(N, D), f(N, D), f(N, D), f(N, D), f(N, D), f(N, (CONV_B - 1) * D), f(N, D), f(N, D),
                 jax.ShapeDtypeStruct((H_B, DH_B, N), BF16), f(N, LANES), f(N, LANES), f(N, LANES), f(N, D),
                 f(N, LANES))
    (a_new, az, v, opre, zb, cb_new, uc, q, ktw, g0, wv, dn, n_new, m_new) = pl.pallas_call(
        _sample_front_body, out_shape=front_out, compiler_params=cparams, name="sample_front",
    )(x_s, st_cb.reshape(N, (CONV_B - 1) * D), st_n.reshape(N, D), m_col, m_row,
      ew['gpre'], ew['win'], ew['wv'], ew['wg'], ew['wgt'], ew['gbr'], ew['gbc'], ew['cwb'], ew['cbb'],
      ew['wq'], ew['wk'], ew['wkt'])

    blk2 = lambda w: pl.BlockSpec((bb, w), lambda i: (i, 0))
    ca_new, conv_out, c_new, numi = pl.pallas_call(
        functools.partial(_sample_state_body, N),
        grid=(NB,),
        in_specs=[pl.BlockSpec((1, bb, nst, D), lambda i: (0, i, 0, 0)), blk2(D), _const_spec(ew['cwa'].shape),
                  pl.BlockSpec((bb, H_B, DH_B, DH_B), lambda i: (i, 0, 0, 0)),
                  blk2(D), _const_spec((N, D)), _const_spec((H_B, DH_B, N)), blk2(LANES)],
        out_specs=(pl.BlockSpec((1, bb, nst, D), lambda i: (0, i, 0, 0)), blk2(D),
                   pl.BlockSpec((bb, H_B, DH_B, DH_B), lambda i: (i, 0, 0, 0)), blk2(D)),
        out_shape=(f(1, N, nst, D), f(N, D), f(N, H_B, DH_B, DH_B), f(N, D)),
        compiler_params=pltpu.CompilerParams(dimension_semantics=("arbitrary",), vmem_limit_bytes=VMEM_LIMIT),
        name="sample_state",
    )(st_ca, a_new, ew['cwa'], st_c, q, v, ktw, g0)

    ts = _rope_tables(np.full((1,), PAST_LEN))
    tabs = (ts['cq'], ts['sq'], ts['ck'], ts['sk'])
    x1, qblk, k_new, v_new, sz = pl.pallas_call(
        functools.partial(_sample_mid_body, NB),
        out_shape=(f(N, D), f(NB, H_C * bb, LANES), f(N, LANES), f(N, LANES), f(N, H_C * LANES)),
        scratch_shapes=[pltpu.VMEM((N, 2 * D), BF16)],
        compiler_params=cparams, name="sample_mid",
    )(x_s, conv_out, az, numi, v, wv, g0, dn, opre, zb, uc,
      ew['cba'], ew['lng'], ew['lnb'], ew['hng'], ew['skip'], ew['wout'], ew['gpost'],
      ow['gpre'], osw['wqs'], ow['wkv'], osw['wzs'], *tabs)

    blk3 = pl.BlockSpec((bb, WINDOW, LANES), lambda i: (i, 0, 0))
    qspec = pl.BlockSpec((1, H_C * bb, LANES), lambda i: (i, 0, 0))
    oblk, kc_new, vc_new = pl.pallas_call(
        _sample_attn_body,
        grid=(NB,),
        in_specs=[qspec, blk3, blk3, blk2(LANES), blk2(LANES), _const_spec((H_C, LANES))],
        out_specs=(qspec, blk3, blk3),
        out_shape=(f(NB, H_C * bb, LANES), f(N, WINDOW, LANES), f(N, WINDOW, LANES)),
        compiler_params=pltpu.CompilerParams(dimension_semantics=("arbitrary",), vmem_limit_bytes=VMEM_LIMIT),
        name="sample_attn",
    )(qblk, ck_win.reshape(N, WINDOW, LANES), cv_win.reshape(N, WINDOW, LANES), k_new, v_new, osw['sinkcol'])

    y_s = pl.pallas_call(
        functools.partial(_sample_back_body, NB),
        out_shape=f(N, D),
        scratch_shapes=[pltpu.VMEM((N, H_C * LANES), F32)],
        compiler_params=cparams, name="sample_back",
    )(oblk, sz, x1, osw['wos'], ow['gpost'])

    return (y_s, ca_new, cb_new.reshape(N, CONV_B - 1, D), c_new,
            n_new.reshape(N, H_B, DH_B), m_new[:, :H_B], kc_new, vc_new)


def _transpose_cast_body(x_ref, o_ref):
    o_ref[...] = x_ref[...].T.astype(BF16)


def _transpose_cast(x):
    r, c = x.shape
    return pl.pallas_call(_transpose_cast_body, out_shape=jax.ShapeDtypeStruct((c, r), BF16),
                          name="transpose_cast")(x)


def _prep_even(j, norm_pre_e, norm_post_e, w_in_e, conv_a_w, conv_a_b, ln_a_g, ln_a_b, conv_b_w, conv_b_b,
               wq_b, wk_b, b_i, b_f, headnorm_b, skip_b, w_out_e):
    D = D_MODEL
    w_in = w_in_e[j]
    wgates = w_in[:, 7 * D:]
    gbias = jnp.concatenate([b_i[j], b_f[j]])
    row2 = lambda a: a.reshape(1, -1)
    return dict(
        gpre=row2(norm_pre_e[j]), gpost=row2(norm_post_e[j]),
        win=jnp.concatenate([w_in[:, :4 * D], w_in[:, 5 * D:7 * D]], axis=1).astype(BF16),
        wv=w_in[:, 4 * D:5 * D].astype(BF16),
        wg=jnp.pad(wgates, ((0, 0), (0, LANES - 2 * H_B))).astype(BF16),
        wgt=_transpose_cast(jnp.pad(wgates, ((0, 0), (0, LANES - 2 * H_B))))[0:2 * H_B, :],
        gbr=jnp.pad(gbias, (0, LANES - 2 * H_B)).reshape(1, LANES),
        gbc=gbias.reshape(2 * H_B, 1),
        cwa=conv_a_w[j], cba=row2(conv_a_b[j]), lng=row2(ln_a_g[j]), lnb=row2(ln_a_b[j]),
        cwb=conv_b_w[j], cbb=row2(conv_b_b[j]),
        wvt=_transpose_cast(w_in[:, 4 * D:5 * D]),
        wq=wq_b[j].astype(BF16), wqt=jnp.swapaxes(wq_b[j], 1, 2).astype(BF16),
        wk=(wk_b[j] * K_SCALE_B).astype(BF16),
        wkt=(jnp.swapaxes(wk_b[j], 1, 2) * K_SCALE_B).astype(BF16),
        hng=row2(headnorm_b[j]), skip=row2(skip_b[j]),
        wout=w_out_e[j].astype(BF16),
    )


def _prep_odd(j, norm_pre_o, norm_post_o, w_in_o, sinks, w_out_o):
    D = D_MODEL
    w_in = w_in_o[j]
    row2 = lambda a: a.reshape(1, -1)
    return dict(
        gpre=row2(norm_pre_o[j]), gpost=row2(norm_post_o[j]),
        wqt=_transpose_cast(w_in[:, :D] * Q_SCALE_C),
        wkv=w_in[:, D:D + 2 * LANES].astype(BF16),
        wz=w_in[:, D + 2 * LANES:].astype(BF16),
        wo=w_out_o[j].astype(BF16),
        sinks=jnp.pad(sinks[j], (0, LANES - H_C)).reshape(1, LANES),
    )


def _prep_odd_sample(j, w_in_o, sinks, w_out_o):
    D = D_MODEL
    w_in = w_in_o[j]
    on_group = (jnp.arange(H_C)[:, None] // G_C) == jnp.arange(N_KV_C)[None, :]

    def stack_cols(w):
        w4 = w.reshape(D, H_C, 1, HD_C)
        return jnp.where(on_group[None, :, :, None], w4, 0.0).reshape(D, H_C * LANES).astype(BF16)
    wo4 = w_out_o[j].reshape(H_C, 1, HD_C, D)
    return dict(
        wqs=stack_cols(w_in[:, :D]),
        wzs=stack_cols(w_in[:, D + 2 * LANES:]),
        wos=jnp.where(on_group[:, :, None, None], wo4, 0.0).reshape(H_C * LANES, D).astype(BF16),
        sinkcol=jnp.broadcast_to(sinks[j][:, None], (H_C, LANES)),
    )


def _rope_tables(pos):
    half = ROT_DIM // 2
    f32 = np.float32
    inv = np.power(f32(ROPE_THETA), -np.arange(half, dtype=f32) * f32(2.0 / ROT_DIM)).astype(f32)
    ang = np.asarray(pos, dtype=f32)[:, None] * inv[None, :]
    cos, sin = np.cos(ang).astype(f32), np.sin(ang).astype(f32)
    n = ang.shape[0]
    pad = HD_C - ROT_DIM
    cos64 = np.concatenate([cos, cos, np.ones((n, pad), f32)], axis=1)
    sin64 = np.concatenate([-sin, sin, np.zeros((n, pad), f32)], axis=1)
    ck = np.concatenate([cos64, cos64], axis=1)
    sk = np.concatenate([sin64, sin64], axis=1)
    tabs = dict(cq=ck * f32(Q_SCALE_C), sq=sk * f32(Q_SCALE_C), ck=ck, sk=sk,
                cos_t=np.ascontiguousarray(cos.T), sin_t=np.ascontiguousarray(sin.T))
    return {k: jnp.asarray(v) for k, v in tabs.items()}


def kernel(x_prompt, x_sample, state_conv_a, state_conv_b, state_mlstm_c, state_mlstm_n, state_mlstm_m, cache_k_win, cache_v_win, norm_pre_e, norm_post_e, w_in_e, conv_a_w, conv_a_b, ln_a_g, ln_a_b, conv_b_w, conv_b_b, wq_b, wk_b, b_i, b_f, headnorm_b, skip_b, w_out_e, norm_pre_o, norm_post_o, w_in_o, sinks, w_out_o):
    B, L, D = x_prompt.shape
    ew = _prep_even(0, norm_pre_e, norm_post_e, w_in_e, conv_a_w, conv_a_b, ln_a_g, ln_a_b, conv_b_w,
                    conv_b_b, wq_b, wk_b, b_i, b_f, headnorm_b, skip_b, w_out_e)
    ow = _prep_odd(0, norm_pre_o, norm_post_o, w_in_o, sinks, w_out_o)

    x1, ca_p, cb_p, c_p, n_p, m_p = _even_prompt(x_prompt, ew, min(TILE_EVEN, L))
    tp = _rope_tables(np.arange(L))
    y_p, kw_p, vw_p = _odd_prompt(x1, ow, (tp['cos_t'], tp['sin_t'], tp['ck'], tp['sk']), min(TILE_ODD, L))

    conv_a_p = ca_p[None, :, 32 - (CONV_A - 1):, :]
    conv_b_p = cb_p[None, :, 8 - (CONV_B - 1):, :]
    mlstm_c_p = c_p[None]
    mlstm_n_p = n_p[None]
    mlstm_m_p = m_p[None, :, :H_B, 0]
    k_win_p = kw_p.reshape(1, B, WINDOW, N_KV_C, HD_C)
    v_win_p = vw_p.reshape(1, B, WINDOW, N_KV_C, HD_C)

    N = x_sample.shape[0]
    osw = _prep_odd_sample(0, w_in_o, sinks, w_out_o)
    y_s, ca_s, cb_s, c_s, n_s, m_s, kw_s, vw_s = _sample_path(
        x_sample.reshape(N, D), state_conv_a, state_conv_b[0], state_mlstm_c[0], state_mlstm_n[0],
        state_mlstm_m[0], cache_k_win[0], cache_v_win[0], ew, ow, osw)
    return (y_p, y_s.reshape(N, 1, D), conv_a_p, conv_b_p, mlstm_c_p, mlstm_n_p, mlstm_m_p, k_win_p, v_win_p,
            ca_s, cb_s[None], c_s[None], n_s[None], m_s[None],
            kw_s.reshape(1, N, WINDOW, N_KV_C, HD_C), vw_s.reshape(1, N, WINDOW, N_KV_C, HD_C))
```

```python
import functools

import jax
import jax.numpy as jnp
import numpy as np
from jax import lax
from jax.experimental import pallas as pl
from jax.experimental.pallas import tpu as pltpu

F32 = jnp.float32
BF16 = jnp.bfloat16

D_MODEL = 1024
EPS = 1e-6
PAST_LEN = 16384
CONV_A = 31
CONV_B = 4
H_B = 4
DH_B = D_MODEL // H_B
HD_C = 64
H_C = D_MODEL // HD_C
N_KV_C = 2
G_C = H_C // N_KV_C
WINDOW = 128
ROT_DIM = HD_C // 4
ROPE_THETA = 500000.0
K_SCALE_B = DH_B ** -0.5
Q_SCALE_C = HD_C ** -0.5

LANES = 128
NG = D_MODEL // LANES
STRIDE = 4
TILE_EVEN = 512
MLSTM_CHUNK = 256
TILE_ODD = 512
ROWS = 16
AUG = 16
SAMPLE_BLOCK = 8
VMEM_LIMIT = 60 * 1024 * 1024

NT_DIMS = (((1,), (1,)), ((), ()))
TN_DIMS = (((0,), (0,)), ((), ()))


def _dot(a, b):
    return jnp.dot(a, b, preferred_element_type=F32)


def _dot_nt(a, b):
    return lax.dot_general(a, b, NT_DIMS, preferred_element_type=F32)


def _sigmoid(x):
    return 1.0 / (1.0 + jnp.exp(-x))


def _silu(x):
    return x * _sigmoid(x)


def _log_sigmoid(x):
    return jnp.minimum(x, 0.0) - jnp.log1p(jnp.exp(-jnp.abs(x)))


def _rms(x, g):
    ms = jnp.mean(x * x, axis=-1, keepdims=True)
    return x * lax.rsqrt(ms + EPS) * g


def _layernorm(x, g, b):
    mu = jnp.mean(x, axis=-1, keepdims=True)
    xc = x - mu
    var = jnp.mean(xc * xc, axis=-1, keepdims=True)
    return xc * lax.rsqrt(var + EPS) * g + b


def _row_loop(total, rows, body):
    for i in range(total // rows):
        body(i * rows)


def _split3(a):
    hi = a.astype(BF16)
    r1 = a - hi.astype(F32)
    mid = r1.astype(BF16)
    lo = (r1 - mid.astype(F32)).astype(BF16)
    return hi, mid, lo


def _conv_strided(src, w_ref, dst, taps, off, total):
    chunk_rows = 8 * STRIDE

    for c in range(total // chunk_rows):
        base = c * chunk_rows
        for g in range(NG):
            gs = slice(g * LANES, (g + 1) * LANES)
            accs = [None] * STRIDE
            rows = [src[g, pl.ds(base + off + s, 8, stride=STRIDE), :] for s in range(taps + STRIDE - 1)]
            for j in range(taps):
                wj = jnp.broadcast_to(w_ref[j:j + 1, gs], (8, LANES))
                for p in range(STRIDE):
                    v = rows[j + p] * wj
                    accs[p] = v if j == 0 else accs[p] + v
            for p in range(STRIDE):
                dst[g, pl.ds(base + p, 8, stride=STRIDE), :] = accs[p]


def _even_prompt_body(T, NT, x_ref, gpre_ref, gpost_ref, win_ref, wvt_ref, wg_ref, gbr_ref,
                      cwa_ref, cba_ref, lng_ref, lnb_ref, cwb_ref, cbb_ref, wqt_ref, wk_ref,
                      hng_ref, skip_ref, wout_ref,
                      y_ref, ca_ref, cb_ref, c_ref, n_ref, m_ref,
                      hn_s, p_s, abuf, ubuf, cbuf, uc_s, ucb_s, ycat_s, cst_s, mst_s,
                      st_s, st2_s, rinv_s, st4_s, rinv4_s, gz_s, vt_s):
    D = D_MODEL
    R = ROWS
    t = pl.program_id(1)

    @pl.when(t == 0)
    def _():
        abuf[:, 0:32, :] = jnp.zeros((NG, 32, LANES), F32)
        ubuf[:, 0:8, :] = jnp.zeros((NG, 8, LANES), F32)
        cst_s[...] = jnp.zeros(cst_s.shape, F32)
        mst_s[...] = jnp.zeros(mst_s.shape, F32)

    _rms_scale(T, lambda sl, gs: x_ref[0, sl, gs], st_s, rinv_s, D)

    def p_norm(r0):
        sl = pl.ds(r0, R)
        rinv = rinv_s[sl, :]
        for g in range(NG):
            gs = slice(g * LANES, (g + 1) * LANES)
            hn_s[sl, gs] = (x_ref[0, sl, gs] * rinv * gpre_ref[:, gs]).astype(BF16)
    _row_loop(T, R, p_norm)

    hn = hn_s[...]

    def proj(p):
        return _dot(hn, win_ref[:, p * D:(p + 1) * D])
    p_s[0] = proj(0)
    p_s[1] = proj(1)
    gz_s[0] = proj(2).astype(BF16)
    u_new = proj(3)
    for g in range(NG):
        ubuf[g, 8:8 + T, :] = u_new[:, g * LANES:(g + 1) * LANES]
    vt_s[...] = _dot_nt(wvt_ref[...], hn).astype(BF16)
    p_s[2] = proj(4)
    gz_s[1] = proj(5).astype(BF16)
    gc = _dot(hn, wg_ref[...]) + gbr_ref[...]

    CH = min(T, MLSTM_CHUNK)
    row = lax.broadcasted_iota(jnp.int32, (CH, CH), 0)
    col = lax.broadcasted_iota(jnp.int32, (CH, CH), 1)
    tril = jnp.where(row >= col, 1.0, 0.0).astype(BF16)
    keep = row <= col

    def p_glu(r0):
        sl = pl.ds(r0, R)
        for g in range(NG):
            gs = slice(g * LANES, (g + 1) * LANES)
            abuf[g, pl.ds(r0 + 32, R), :] = p_s[0, sl, gs] * _sigmoid(p_s[1, sl, gs])
    _row_loop(T, R, p_glu)

    _conv_strided(abuf, cwa_ref, cbuf, CONV_A, 32 - (CONV_A - 1), T)

    ones = jnp.ones((LANES, LANES), BF16)

    def p_ln_mean(r0):
        sl = pl.ds(r0, R)
        acc = None
        for g in range(NG):
            x = cbuf[g, sl, :] + cba_ref[:, g * LANES:(g + 1) * LANES]
            cbuf[g, sl, :] = x
            acc = x if acc is None else acc + x
        hi = acc.astype(BF16)
        st_s[sl, :] = hi
        st2_s[sl, :] = (acc - hi.astype(F32)).astype(BF16)
    _row_loop(T, R, p_ln_mean)
    rinv_s[...] = (_dot(st_s[...], ones) + _dot(st2_s[...], ones)) * (1.0 / D)

    def p_ln_center(r0):
        sl = pl.ds(r0, R)
        mu = rinv_s[sl, :]
        acc = None
        for g in range(NG):
            xc = cbuf[g, sl, :] - mu
            cbuf[g, sl, :] = xc
            acc = xc * xc if acc is None else acc + xc * xc
        st_s[sl, :] = acc.astype(BF16)
    _row_loop(T, R, p_ln_center)
    rinv_s[...] = lax.rsqrt(_dot(st_s[...], ones) * (1.0 / D) + EPS)

    def p_ln(r0):
        sl = pl.ds(r0, R)
        rs = rinv_s[sl, :]
        for g in range(NG):
            gs = slice(g * LANES, (g + 1) * LANES)
            yv = cbuf[g, sl, :] * rs * lng_ref[:, gs] + lnb_ref[:, gs]
            ycat_s[sl, gs] = _silu(yv.astype(BF16)) * _silu(gz_s[0, sl, gs])
    _row_loop(T, R, p_ln)

    _conv_strided(ubuf, cwb_ref, cbuf, CONV_B, 8 - (CONV_B - 1), T)

    def p_uc(r0):
        sl = pl.ds(r0, R)
        for g in range(NG):
            gs = slice(g * LANES, (g + 1) * LANES)
            uc = _silu(cbuf[g, sl, :] + cbb_ref[:, gs])
            uc_s[sl, gs] = uc
            ucb_s[sl, gs] = uc.astype(BF16)
    _row_loop(T, R, p_uc)

    ones_rows = jnp.ones((AUG, CH), BF16)
    for c in range(T // CH):
        rs = slice(c * CH, (c + 1) * CH)
        gcc = gc[rs, :]
        ch, cm, cl = _split3(_log_sigmoid(gcc))
        b_c = _dot(tril, ch) + _dot(tril, cm) + _dot(tril, cl)
        b_t = b_c.T
        for h in range(H_B):
            hs = slice(h * DH_B, (h + 1) * DH_B)
            ub = ucb_s[rs, hs]
            kb = _dot(ub, wk_ref[h])
            qt = _dot_nt(wqt_ref[h], ub).astype(BF16)
            st = _dot(kb.astype(BF16), qt)
            u_c = gcc[:, h:h + 1] - b_c[:, H_B + h:H_B + h + 1]
            b_r = b_t[H_B + h:H_B + h + 1, :]
            m_h = mst_s[h:h + 1, 0:1]
            dm = jnp.where(keep, b_r + u_c, -jnp.inf)
            inter = b_r + m_h
            mt = jnp.maximum(inter, jnp.max(dm, axis=0, keepdims=True))
            wt = (st * jnp.exp(dm - mt)).astype(BF16)
            a_in = jnp.exp(inter - mt)
            vta = jnp.concatenate([vt_s[hs, rs], ones_rows], axis=0)
            cta = cst_s[h]
            numt = _dot(vta, wt) + a_in * _dot(cta.astype(BF16), qt)
            den = numt[DH_B:DH_B + 1, :]
            ht = numt[0:DH_B, :] * (1.0 / jnp.maximum(jnp.abs(den), jnp.exp(-mt)))
            p_s[0, rs, hs] = ht.T
            bl = b_r[:, CH - 1:CH]
            wl = bl + u_c
            m_new = jnp.maximum(bl + m_h, jnp.max(wl, axis=0, keepdims=True))
            g0 = jnp.exp(bl + m_h - m_new)
            kw = (kb * jnp.exp(wl - m_new)).astype(BF16)
            cst_s[h] = g0 * cta + _dot(vta, kw)
            mst_s[h:h + 1, :] = jnp.broadcast_to(m_new, (1, LANES))

    gph = DH_B // LANES

    def p_ogate(r0):
        sl = pl.ds(r0, R)
        for h in range(H_B):
            acc = None
            for gg in range(gph):
                gs = slice((h * gph + gg) * LANES, (h * gph + gg + 1) * LANES)
                o = _sigmoid(p_s[2, sl, gs]) * p_s[0, sl, gs]
                p_s[0, sl, gs] = o
                acc = o * o if acc is None else acc + o * o
            st4_s[sl, h * LANES:(h + 1) * LANES] = acc.astype(BF16)
    _row_loop(T, R, p_ogate)
    for h in range(H_B):
        hl = slice(h * LANES, (h + 1) * LANES)
        rinv4_s[:, hl] = lax.rsqrt(_dot(st4_s[:, hl], ones) * (1.0 / DH_B) + EPS)

    def p_hb(r0):
        sl = pl.ds(r0, R)
        for g in range(NG):
            gs = slice(g * LANES, (g + 1) * LANES)
            h = g // gph
            hb = p_s[0, sl, gs] * rinv4_s[sl, h * LANES:(h + 1) * LANES] * hng_ref[:, gs]
            hb = (hb + skip_ref[:, gs] * uc_s[sl, gs]).astype(BF16) * _silu(gz_s[1, sl, gs])
            ycat_s[sl, D + g * LANES:D + (g + 1) * LANES] = hb
    _row_loop(T, R, p_hb)

    p_s[1] = _dot(ycat_s[...], wout_ref[...])
    _rms_scale(T, lambda sl, gs: p_s[1, sl, gs], st_s, rinv_s, D)

    def p_out(r0):
        sl = pl.ds(r0, R)
        rinv = rinv_s[sl, :]
        for g in range(NG):
            gs = slice(g * LANES, (g + 1) * LANES)
            y_ref[0, sl, gs] = x_ref[0, sl, gs] + p_s[1, sl, gs] * rinv * gpost_ref[:, gs]
    _row_loop(T, R, p_out)

    @pl.when(t == NT - 1)
    def _():
        for g in range(NG):
            ca_ref[0, :, g * LANES:(g + 1) * LANES] = abuf[g, T:T + 32, :]
            cb_ref[0, :, g * LANES:(g + 1) * LANES] = ubuf[g, T:T + 8, :]
        for h in range(H_B):
            c_ref[0, h] = cst_s[h, 0:DH_B, :].T
            n_ref[0, h:h + 1, :] = cst_s[h, DH_B:DH_B + 1, :]
        m_ref[0] = mst_s[...]

    abuf[:, 0:32, :] = abuf[:, T:T + 32, :]
    ubuf[:, 0:8, :] = ubuf[:, T:T + 8, :]


def _const_spec(shape):
    nd = len(shape)
    return pl.BlockSpec(shape, lambda *_: (0,) * nd, pipeline_mode=pl.Buffered(1))


def _even_prompt(x, ew, T):
    B, L, D = x.shape
    NT = L // T
    consts = (ew['gpre'], ew['gpost'], ew['win'], ew['wvt'], ew['wg'], ew['gbr'],
              ew['cwa'], ew['cba'], ew['lng'], ew['lnb'], ew['cwb'], ew['cbb'], ew['wqt'], ew['wk'],
              ew['hng'], ew['skip'], ew['wout'])
    in_specs = [pl.BlockSpec((1, T, D), lambda b, t: (b, t, 0))] + [_const_spec(c.shape) for c in consts]
    out_shape = (
        jax.ShapeDtypeStruct((B, L, D), F32),
        jax.ShapeDtypeStruct((B, 32, D), F32),
        jax.ShapeDtypeStruct((B, 8, D), F32),
        jax.ShapeDtypeStruct((B, H_B, DH_B, DH_B), F32),
        jax.ShapeDtypeStruct((B, H_B, DH_B), F32),
        jax.ShapeDtypeStruct((B, 8, LANES), F32),
    )
    out_specs = (
        pl.BlockSpec((1, T, D), lambda b, t: (b, t, 0)),
        pl.BlockSpec((1, 32, D), lambda b, t: (b, 0, 0)),
        pl.BlockSpec((1, 8, D), lambda b, t: (b, 0, 0)),
        pl.BlockSpec((1, H_B, DH_B, DH_B), lambda b, t: (b, 0, 0, 0)),
        pl.BlockSpec((1, H_B, DH_B), lambda b, t: (b, 0, 0)),
        pl.BlockSpec((1, 8, LANES), lambda b, t: (b, 0, 0)),
    )
    scratch = [
        pltpu.VMEM((T, D), BF16),
        pltpu.VMEM((3, T, D), F32),
        pltpu.VMEM((NG, T + 32, LANES), F32),
        pltpu.VMEM((NG, T + 8, LANES), F32),
        pltpu.VMEM((NG, T, LANES), F32),
        pltpu.VMEM((T, D), F32),
        pltpu.VMEM((T, D), BF16),
        pltpu.VMEM((T, 2 * D), BF16),
        pltpu.VMEM((H_B, DH_B + AUG, DH_B), F32),
        pltpu.VMEM((8, LANES), F32),
        pltpu.VMEM((T, LANES), BF16),
        pltpu.VMEM((T, LANES), BF16),
        pltpu.VMEM((T, LANES), F32),
        pltpu.VMEM((T, H_B * LANES), BF16),
        pltpu.VMEM((T, H_B * LANES), F32),
        pltpu.VMEM((2, T, D), BF16),
        pltpu.VMEM((D, T), BF16),
    ]
    return pl.pallas_call(
        functools.partial(_even_prompt_body, T, NT),
        grid=(B, NT),
        in_specs=in_specs,
        out_specs=out_specs,
        out_shape=out_shape,
        scratch_shapes=scratch,
        compiler_params=pltpu.CompilerParams(
            dimension_semantics=("arbitrary", "arbitrary"), vmem_limit_bytes=VMEM_LIMIT),
        name="even_prompt",
    )(x, *consts)


def _rope_slab(x, cos, sin, lane_lo):
    partner = jnp.where(lane_lo, pltpu.roll(x, LANES - ROT_DIM // 2, 1), pltpu.roll(x, ROT_DIM // 2, 1))
    return x * cos + partner * sin


def _rms_scale(T, src, st_s, rinv_s, width):
    R = ROWS
    ng = width // LANES

    def p_sq(r0):
        sl = pl.ds(r0, R)
        acc = None
        for g in range(ng):
            xg = src(sl, slice(g * LANES, (g + 1) * LANES))
            acc = xg * xg if acc is None else acc + xg * xg
        st_s[sl, :] = acc.astype(BF16)
    _row_loop(T, R, p_sq)
    ms = _dot(st_s[...], jnp.ones((LANES, LANES), BF16))
    rinv_s[...] = lax.rsqrt(ms * (1.0 / width) + EPS)


def _odd_prompt_body(T, NT, x_ref, gpre_ref, gpost_ref, wqt_ref, wkv_ref, wz_ref, wo_ref, sinks_ref,
                     cqt_ref, sqt_ref, ck_ref, sk_ref,
                     y_ref, kw_ref, vw_ref,
                     hn_s, st_s, rinv_s, py_s, pz_s, pkv_s, qt_s, kr_s, kb_s, vf_s, o_s, g_s):
    D = D_MODEL
    R = ROWS
    W = WINDOW
    t = pl.program_id(1)

    @pl.when(t == 0)
    def _():
        kb_s[0:W, :] = jnp.zeros((W, LANES), BF16)
        vf_s[:, 0:W, :] = jnp.zeros((4, W, LANES), BF16)

    _rms_scale(T, lambda sl, gs: x_ref[0, sl, gs], st_s, rinv_s, D)

    def p_norm(r0):
        sl = pl.ds(r0, R)
        rinv = rinv_s[sl, :]
        for g in range(NG):
            gs = slice(g * LANES, (g + 1) * LANES)
            hn_s[sl, gs] = (x_ref[0, sl, gs] * rinv * gpre_ref[:, gs]).astype(BF16)
    _row_loop(T, R, p_norm)

    hn = hn_s[...]
    qt = _dot_nt(wqt_ref[...], hn)
    pkv_s[...] = _dot(hn, wkv_ref[...])
    pz_s[...] = _dot(hn, wz_ref[...]).astype(BF16)

    cq, sq = cqt_ref[...], sqt_ref[...]
    half = ROT_DIM // 2
    for h in range(H_C):
        b0 = h * HD_C
        x1 = qt[b0:b0 + half, :]
        x2 = qt[b0 + half:b0 + ROT_DIM, :]
        rot = jnp.concatenate([x1 * cq - x2 * sq, x2 * cq + x1 * sq, qt[b0 + ROT_DIM:b0 + HD_C, :]], axis=0)
        qt_s[b0:b0 + HD_C, :] = rot.astype(BF16)

    RK = 64
    for c in range(T // RK):
        sl = slice(c * RK, (c + 1) * RK)
        lane = lax.broadcasted_iota(jnp.int32, (RK, LANES), 1)
        lane_lo = (lane % HD_C) < (ROT_DIM // 2)
        head_lo = lane < HD_C
        kr = _rope_slab(pkv_s[sl, 0:LANES], ck_ref[sl, :], sk_ref[sl, :], lane_lo)
        kr_s[sl, :] = kr
        vr = pkv_s[sl, LANES:2 * LANES]
        vrr = pltpu.roll(vr, HD_C, 1)
        dst = slice(W + c * RK, W + (c + 1) * RK)
        zero = jnp.zeros_like(kr)
        kb_s[dst, :] = kr.astype(BF16)
        vf_s[0, dst, :] = jnp.where(head_lo, vr, zero).astype(BF16)
        vf_s[1, dst, :] = jnp.where(head_lo, zero, vrr).astype(BF16)
        vf_s[2, dst, :] = jnp.where(head_lo, vrr, zero).astype(BF16)
        vf_s[3, dst, :] = jnp.where(head_lo, zero, vr).astype(BF16)

    kj = lax.broadcasted_iota(jnp.int32, (2 * W, W), 0)
    qi = lax.broadcasted_iota(jnp.int32, (2 * W, W), 1)
    band = (kj >= qi) & (kj <= qi + W)
    for i in range(T // W):
        rows = slice(i * W, (i + 1) * W)
        krows = slice(i * W, i * W + 2 * W)
        if i == 0:
            valid = band & ((kj >= W) | (t > 0))
        else:
            valid = band
        bias = jnp.where(valid, 0.0, -jnp.inf)
        bias2 = jnp.concatenate([bias, bias], axis=1)
        first_head = lax.broadcasted_iota(jnp.int32, (1, 2 * W), 1) < W
        kk = kb_s[krows, :]
        zpad = jnp.zeros((HD_C, 2 * W), BF16)
        for g in range(N_KV_C):
            vv2 = jnp.concatenate([vf_s[2 * g, krows, :], vf_s[2 * g + 1, krows, :]], axis=0)
            for pp in range(G_C // 2):
                pair = g * (G_C // 2) + pp
                ls = slice(pair * LANES, (pair + 1) * LANES)
                qq = jnp.concatenate([qt_s[pair * LANES:pair * LANES + HD_C, rows],
                                      qt_s[pair * LANES + HD_C:(pair + 1) * LANES, rows]], axis=1)
                rhs = jnp.concatenate([qq, zpad] if g == 0 else [zpad, qq], axis=0)
                sh = _dot(kk, rhs) + bias2
                sink = jnp.where(first_head, sinks_ref[0:1, 2 * pair:2 * pair + 1],
                                 sinks_ref[0:1, 2 * pair + 1:2 * pair + 2])
                m = jnp.maximum(jnp.max(sh, axis=0, keepdims=True), sink)
                pe = jnp.exp(sh - m)
                den = jnp.sum(pe, axis=0, keepdims=True) + jnp.exp(sink - m)
                p = (pe * (1.0 / den)).astype(BF16)
                p2t = jnp.concatenate([p[:, 0:W], p[:, W:2 * W]], axis=0)
                o_s[rows, ls] = lax.dot_general(p2t, vv2, TN_DIMS, preferred_element_type=F32)

    def p_gate(r0):
        sl = pl.ds(r0, R)
        g_s[sl, :] = o_s[sl, :].astype(BF16) * _silu(pz_s[sl, :])
    _row_loop(T, R, p_gate)

    py_s[...] = _dot(g_s[...], wo_ref[...])
    _rms_scale(T, lambda sl, gs: py_s[sl, gs], st_s, rinv_s, D)

    def p_out(r0):
        sl = pl.ds(r0, R)
        rinv = rinv_s[sl, :]
        for g in range(NG):
            gs = slice(g * LANES, (g + 1) * LANES)
            y_ref[0, sl, gs] = x_ref[0, sl, gs] + py_s[sl, gs] * rinv * gpost_ref[:, gs]
    _row_loop(T, R, p_out)

    @pl.when(t == NT - 1)
    def _():
        kw_ref[0] = kr_s[T - W:T, :]
        vw_ref[0] = pkv_s[T - W:T, LANES:2 * LANES]

    kb_s[0:W, :] = kb_s[T:T + W, :]
    vf_s[:, 0:W, :] = vf_s[:, T:T + W, :]


def _odd_prompt(x, ow, tabs, T):
    B, L, D = x.shape
    NT = L // T
    consts = (ow['gpre'], ow['gpost'], ow['wqt'], ow['wkv'], ow['wz'], ow['wo'], ow['sinks'])
    tab_spec = pl.BlockSpec((T, LANES), lambda b, t: (t, 0))
    tabt_spec = pl.BlockSpec((ROT_DIM // 2, T), lambda b, t: (0, t))
    in_specs = ([pl.BlockSpec((1, T, D), lambda b, t: (b, t, 0))] + [_const_spec(c.shape) for c in consts]
                + [tabt_spec] * 2 + [tab_spec] * 2)
    out_shape = (
        jax.ShapeDtypeStruct((B, L, D), F32),
        jax.ShapeDtypeStruct((B, WINDOW, LANES), F32),
        jax.ShapeDtypeStruct((B, WINDOW, LANES), F32),
    )
    out_specs = (
        pl.BlockSpec((1, T, D), lambda b, t: (b, t, 0)),
        pl.BlockSpec((1, WINDOW, LANES), lambda b, t: (b, 0, 0)),
        pl.BlockSpec((1, WINDOW, LANES), lambda b, t: (b, 0, 0)),
    )
    scratch = [
        pltpu.VMEM((T, D), BF16),
        pltpu.VMEM((T, LANES), BF16),
        pltpu.VMEM((T, LANES), F32),
        pltpu.VMEM((T, D), F32),
        pltpu.VMEM((T, D), BF16),
        pltpu.VMEM((T, 2 * LANES), F32),
        pltpu.VMEM((D, T), BF16),
        pltpu.VMEM((T, LANES), F32),
        pltpu.VMEM((T + WINDOW, LANES), BF16),
        pltpu.VMEM((4, T + WINDOW, LANES), BF16),
        pltpu.VMEM((T, D), F32),
        pltpu.VMEM((T, D), BF16),
    ]
    return pl.pallas_call(
        functools.partial(_odd_prompt_body, T, NT),
        grid=(B, NT),
        in_specs=in_specs,
        out_specs=out_specs,
        out_shape=out_shape,
        scratch_shapes=scratch,
        compiler_params=pltpu.CompilerParams(
            dimension_semantics=("arbitrary", "arbitrary"), vmem_limit_bytes=VMEM_LIMIT),
        name="odd_prompt",
    )(x, *consts, *tabs)


def _sample_front_body(x_ref, cbs_ref, n_ref, mc_ref, mr_ref, gpre_ref, win_ref, wvp_ref, wg_ref, wgt_ref, gbr_ref,
                       gbc_ref, cwb_ref, cbb_ref, wq_ref, wk_ref, wkt_ref,
                       anew_ref, az_ref, v_ref, op_ref, zb_ref, cbo_ref, uc_ref, q_ref, ktw_ref,
                       g0_ref, wv_ref, dn_ref, nn_ref, mn_ref):
    D = D_MODEL
    hn = _rms(x_ref[...], gpre_ref[...]).astype(BF16)

    def proj(p):
        return _dot(hn, win_ref[:, p * D:(p + 1) * D])
    anew_ref[...] = proj(0) * _sigmoid(proj(1))
    az_ref[...] = proj(2)
    u = proj(3)
    v_ref[...] = _dot(hn, wvp_ref[...])
    op_ref[...] = proj(4)
    zb_ref[...] = proj(5)
    gc = _dot(hn, wg_ref[...]) + gbr_ref[...]
    gr = _dot_nt(wgt_ref[...], hn) + gbc_ref[...]

    acc = u * cwb_ref[CONV_B - 1:CONV_B, :]
    for j in range(CONV_B - 1):
        acc = acc + cbs_ref[:, j * D:(j + 1) * D] * cwb_ref[j:j + 1, :]
    uc = _silu(acc + cbb_ref[...])
    uc_ref[...] = uc
    cbo_ref[:, 0:(CONV_B - 2) * D] = cbs_ref[:, D:(CONV_B - 1) * D]
    cbo_ref[:, (CONV_B - 2) * D:] = u

    ucb = uc.astype(BF16)
    g0_ref[...] = jnp.zeros(g0_ref.shape, F32)
    wv_ref[...] = jnp.zeros(wv_ref.shape, F32)
    dn_ref[...] = jnp.ones(dn_ref.shape, F32)
    mn_ref[...] = jnp.zeros(mn_ref.shape, F32)
    for h in range(H_B):
        hs = slice(h * DH_B, (h + 1) * DH_B)
        ub = ucb[:, hs]
        q = _dot(ub, wq_ref[h])
        k = _dot(ub, wk_ref[h])
        kt = _dot_nt(wkt_ref[h], ub)
        q_ref[:, hs] = q
        nrow = n_ref[:, hs]
        li_c = gc[:, h:h + 1]
        lf_c = _log_sigmoid(gc[:, H_B + h:H_B + h + 1])
        m_c = mc_ref[:, h:h + 1]
        m_new = jnp.maximum(lf_c + m_c, li_c)
        g0 = jnp.exp(lf_c + m_c - m_new)
        ws = jnp.exp(li_c - m_new)
        w = jnp.sum(q * k, axis=1, keepdims=True) * ws
        den = w + g0 * jnp.sum(q * nrow, axis=1, keepdims=True)
        g0_ref[:, h:h + 1] = g0
        wv_ref[:, h:h + 1] = w
        dn_ref[:, h:h + 1] = jnp.maximum(jnp.abs(den), jnp.exp(-m_new))
        mn_ref[:, h:h + 1] = m_new
        nn_ref[:, hs] = g0 * nrow + ws * k
        li_r = gr[h:h + 1, :]
        lf_r = _log_sigmoid(gr[H_B + h:H_B + h + 1, :])
        m_r = mr_ref[h:h + 1, :]
        ws_r = jnp.exp(li_r - jnp.maximum(lf_r + m_r, li_r))
        ktw_ref[h] = (kt * ws_r).astype(BF16)


def _sample_state_body(N, ca_ref, anew_ref, cwa_ref, c_ref, q_ref, v_ref, ktw_ref, g0_ref,
                       cao_ref, co_ref, cn_ref, numi_ref):
    D = D_MODEL
    bb = SAMPLE_BLOCK
    i = pl.program_id(0)
    nst = CONV_A - 1
    for j in range(bb):
        a_new = anew_ref[j:j + 1, :]
        past = jnp.sum(ca_ref[0, j] * cwa_ref[0:nst, :], axis=0, keepdims=True)
        co_ref[j:j + 1, :] = past + a_new * cwa_ref[nst:nst + 1, :]
        cao_ref[0, j, 0:nst - 1, :] = ca_ref[0, j, 1:nst, :]
        cao_ref[0, j, nst - 1:nst, :] = a_new

    rown = lax.broadcasted_iota(jnp.int32, (N, DH_B), 0)
    rowb = lax.broadcasted_iota(jnp.int32, (bb, DH_B), 0)
    for h in range(H_B):
        hs = slice(h * DH_B, (h + 1) * DH_B)
        qh = q_ref[:, hs].astype(BF16)
        vh = v_ref[:, hs]
        ktw = ktw_ref[h]
        numi = jnp.zeros((bb, DH_B), F32)
        for j in range(bb):
            cm = c_ref[j, h]
            r = _dot(qh, cm.astype(BF16))
            numi = jnp.where(rowb == j, r, numi)
            vsel = jnp.where(rown == i * bb + j, vh, 0.0).astype(BF16)
            cn_ref[j, h] = g0_ref[j:j + 1, h:h + 1] * cm + _dot(ktw, vsel)
        numi_ref[:, hs] = numi


def _sample_mid_body(NB, x_ref, co_ref, az_ref, numi_ref, v_ref, wv_ref, g0_ref, dn_ref, op_ref, zb_ref, uc_ref,
                     cba_ref, lng_ref, lnb_ref, hng_ref, skip_ref, wout_ref, gpost_ref,
                     gpre_ref, wqs_ref, wkv_ref, wzs_ref, cq_ref, sq_ref, ck_ref, sk_ref,
                     x1_ref, qblk_ref, kn_ref, vn_ref, sz_ref, ycat_s):
    D = D_MODEL
    bb = SAMPLE_BLOCK
    ya = _silu(_layernorm(co_ref[...] + cba_ref[...], lng_ref[...], lnb_ref[...])) * _silu(az_ref[...])
    ycat_s[:, 0:D] = ya.astype(BF16)
    for h in range(H_B):
        hs = slice(h * DH_B, (h + 1) * DH_B)
        num = wv_ref[:, h:h + 1] * v_ref[:, hs] + g0_ref[:, h:h + 1] * numi_ref[:, hs]
        o = _sigmoid(op_ref[:, hs]) * (num / dn_ref[:, h:h + 1])
        hb = _rms(o, hng_ref[:, hs])
        hb = (hb + skip_ref[:, hs] * uc_ref[:, hs]) * _silu(zb_ref[:, hs])
        ycat_s[:, D + h * DH_B:D + (h + 1) * DH_B] = hb.astype(BF16)
    x1 = x_ref[...] + _rms(_dot(ycat_s[...], wout_ref[...]), gpost_ref[...])
    x1_ref[...] = x1

    hn = _rms(x1, gpre_ref[...]).astype(BF16)
    lane = lax.broadcasted_iota(jnp.int32, (x1.shape[0], LANES), 1)
    lane_lo = (lane % HD_C) < (ROT_DIM // 2)
    kv = _dot(hn, wkv_ref[...])
    kn_ref[...] = _rope_slab(kv[:, 0:LANES], ck_ref[...], sk_ref[...], lane_lo)
    vn_ref[...] = kv[:, LANES:2 * LANES]
    sz_ref[...] = _silu(_dot(hn, wzs_ref[...]))
    qs = _dot(hn, wqs_ref[...])
    for h in range(H_C):
        qh = _rope_slab(qs[:, h * LANES:(h + 1) * LANES], cq_ref[...], sq_ref[...], lane_lo)
        for blk in range(NB):
            qblk_ref[blk, h * bb:(h + 1) * bb, :] = qh[blk * bb:(blk + 1) * bb, :]


def _sample_attn_body(q_ref, kc_ref, vc_ref, kn_ref, vn_ref, sink_ref, o_ref, kco_ref, vco_ref):
    bb = SAMPLE_BLOCK
    W = WINDOW
    qs = [q_ref[0, pl.ds(j, H_C, stride=bb), :] for j in range(bb)]
    s = jnp.concatenate([_dot_nt(qs[j].astype(BF16), kc_ref[j].astype(BF16)) for j in range(bb)], axis=0)
    sn = jnp.concatenate([jnp.sum(qs[j] * kn_ref[j:j + 1, :], axis=1, keepdims=True) for j in range(bb)], axis=0)
    sink = jnp.concatenate([sink_ref[:, 0:1]] * bb, axis=0)
    m = jnp.maximum(jnp.maximum(jnp.max(s, axis=1, keepdims=True), sn), sink)
    p = jnp.exp(s - m)
    pn = jnp.exp(sn - m)
    rden = 1.0 / (jnp.sum(p, axis=1, keepdims=True) + pn + jnp.exp(sink - m))
    pb = (p * rden).astype(BF16)
    pn = pn * rden
    for j in range(bb):
        rows = slice(j * H_C, (j + 1) * H_C)
        kn = kn_ref[j:j + 1, :]
        vn = vn_ref[j:j + 1, :]
        o_ref[0, pl.ds(j, H_C, stride=bb), :] = _dot(pb[rows, :], vc_ref[j].astype(BF16)) + pn[rows, :] * vn
        kco_ref[j, 0:W - 1, :] = kc_ref[j, 1:W, :]
        kco_ref[j, W - 1:W, :] = kn
        vco_ref[j, 0:W - 1, :] = vc_ref[j, 1:W, :]
        vco_ref[j, W - 1:W, :] = vn


def _sample_back_body(NB, o_ref, sz_ref, x1_ref, wos_ref, gpost_ref, y_ref, g_s):
    bb = SAMPLE_BLOCK
    for blk in range(NB):
        rs = slice(blk * bb, (blk + 1) * bb)
        for h in range(H_C):
            ls = slice(h * LANES, (h + 1) * LANES)
            g_s[rs, ls] = o_ref[blk, h * bb:(h + 1) * bb, :] * sz_ref[rs, ls]
    y = _dot(g_s[...].astype(BF16), wos_ref[...])
    y_ref[...] = x1_ref[...] + _rms(y, gpost_ref[...])


def _sample_path(x_s, st_ca, st_cb, st_c, st_n, st_m, ck_win, cv_win, ew, ow, osw):
    N, D = x_s.shape
    bb = SAMPLE_BLOCK
    NB = N // bb
    nst = CONV_A - 1
    f = lambda *shape: jax.ShapeDtypeStruct(shape, F32)
    cparams = pltpu.CompilerParams(vmem_limit_bytes=VMEM_LIMIT)

    m_col = jnp.pad(st_m, ((0, 0), (0, LANES - H_B)))
    m_row = jnp.pad(st_m.T, ((0, 8 - H_B), (0, 0)))
    front_out = (f(N, D), f(N, D), f(N, D), f(N, D), f(N, D), f(N, (CONV_B - 1) * D), f(N, D), f(N, D),
                 jax.ShapeDtypeStruct((H_B, DH_B, N), BF16), f(N, LANES), f(N, LANES), f(N, LANES), f(N, D),
                 f(N, LANES))
    (a_new, az, v, opre, zb, cb_new, uc, q, ktw, g0, wv, dn, n_new, m_new) = pl.pallas_call(
        _sample_front_body, out_shape=front_out, compiler_params=cparams, name="sample_front",
    )(x_s, st_cb.reshape(N, (CONV_B - 1) * D), st_n.reshape(N, D), m_col, m_row,
      ew['gpre'], ew['win'], ew['wv'], ew['wg'], ew['wgt'], ew['gbr'], ew['gbc'], ew['cwb'], ew['cbb'],
      ew['wq'], ew['wk'], ew['wkt'])

    blk2 = lambda w: pl.BlockSpec((bb, w), lambda i: (i, 0))
    ca_new, conv_out, c_new, numi = pl.pallas_call(
        functools.partial(_sample_state_body, N),
        grid=(NB,),
        in_specs=[pl.BlockSpec((1, bb, nst, D), lambda i: (0, i, 0, 0)), blk2(D), _const_spec(ew['cwa'].shape),
                  pl.BlockSpec((bb, H_B, DH_B, DH_B), lambda i: (i, 0, 0, 0)),
                  blk2(D), _const_spec((N, D)), _const_spec((H_B, DH_B, N)), blk2(LANES)],
        out_specs=(pl.BlockSpec((1, bb, nst, D), lambda i: (0, i, 0, 0)), blk2(D),
                   pl.BlockSpec((bb, H_B, DH_B, DH_B), lambda i: (i, 0, 0, 0)), blk2(D)),
        out_shape=(f(1, N, nst, D), f(N, D), f(N, H_B, DH_B, DH_B), f(N, D)),
        compiler_params=pltpu.CompilerParams(dimension_semantics=("arbitrary",), vmem_limit_bytes=VMEM_LIMIT),
        name="sample_state",
    )(st_ca, a_new, ew['cwa'], st_c, q, v, ktw, g0)

    ts = _rope_tables(np.full((1,), PAST_LEN))
    tabs = (ts['cq'], ts['sq'], ts['ck'], ts['sk'])
    x1, qblk, k_new, v_new, sz = pl.pallas_call(
        functools.partial(_sample_mid_body, NB),
        out_shape=(f(N, D), f(NB, H_C * bb, LANES), f(N, LANES), f(N, LANES), f(N, H_C * LANES)),
        scratch_shapes=[pltpu.VMEM((N, 2 * D), BF16)],
        compiler_params=cparams, name="sample_mid",
    )(x_s, conv_out, az, numi, v, wv, g0, dn, opre, zb, uc,
      ew['cba'], ew['lng'], ew['lnb'], ew['hng'], ew['skip'], ew['wout'], ew['gpost'],
      ow['gpre'], osw['wqs'], ow['wkv'], osw['wzs'], *tabs)

    blk3 = pl.BlockSpec((bb, WINDOW, LANES), lambda i: (i, 0, 0))
    qspec = pl.BlockSpec((1, H_C * bb, LANES), lambda i: (i, 0, 0))
    oblk, kc_new, vc_new = pl.pallas_call(
        _sample_attn_body,
        grid=(NB,),
        in_specs=[qspec, blk3, blk3, blk2(LANES), blk2(LANES), _const_spec((H_C, LANES))],
        out_specs=(qspec, blk3, blk3),
        out_shape=(f(NB, H_C * bb, LANES), f(N, WINDOW, LANES), f(N, WINDOW, LANES)),
        compiler_params=pltpu.CompilerParams(dimension_semantics=("arbitrary",), vmem_limit_bytes=VMEM_LIMIT),
        name="sample_attn",
    )(qblk, ck_win.reshape(N, WINDOW, LANES), cv_win.reshape(N, WINDOW, LANES), k_new, v_new, osw['sinkcol'])

    y_s = pl.pallas_call(
        functools.partial(_sample_back_body, NB),
        out_shape=f(N, D),
        scratch_shapes=[pltpu.VMEM((N, H_C * LANES), F32)],
        compiler_params=cparams, name="sample_back",
    )(oblk, sz, x1, osw['wos'], ow['gpost'])

    return (y_s, ca_new, cb_new.reshape(N, CONV_B - 1, D), c_new,
            n_new.reshape(N, H_B, DH_B), m_new[:, :H_B], kc_new, vc_new)


def _prep_even(j, norm_pre_e, norm_post_e, w_in_e, conv_a_w, conv_a_b, ln_a_g, ln_a_b, conv_b_w, conv_b_b,
               wq_b, wk_b, b_i, b_f, headnorm_b, skip_b, w_out_e):
    D = D_MODEL
    w_in = w_in_e[j]
    wgates = w_in[:, 7 * D:]
    gbias = jnp.concatenate([b_i[j], b_f[j]])
    row2 = lambda a: a.reshape(1, -1)
    return dict(
        gpre=row2(norm_pre_e[j]), gpost=row2(norm_post_e[j]),
        win=jnp.concatenate([w_in[:, :4 * D], w_in[:, 5 * D:7 * D]], axis=1).astype(BF16),
        wv=w_in[:, 4 * D:5 * D].astype(BF16),
        wg=jnp.pad(wgates, ((0, 0), (0, LANES - 2 * H_B))).astype(BF16),
        wgt=wgates.T.astype(BF16),
        gbr=jnp.pad(gbias, (0, LANES - 2 * H_B)).reshape(1, LANES),
        gbc=gbias.reshape(2 * H_B, 1),
        cwa=conv_a_w[j], cba=row2(conv_a_b[j]), lng=row2(ln_a_g[j]), lnb=row2(ln_a_b[j]),
        cwb=conv_b_w[j], cbb=row2(conv_b_b[j]),
        wvt=w_in[:, 4 * D:5 * D].T.astype(BF16),
        wq=wq_b[j].astype(BF16), wqt=jnp.swapaxes(wq_b[j], 1, 2).astype(BF16),
        wk=(wk_b[j] * K_SCALE_B).astype(BF16),
        wkt=(jnp.swapaxes(wk_b[j], 1, 2) * K_SCALE_B).astype(BF16),
        hng=row2(headnorm_b[j]), skip=row2(skip_b[j]),
        wout=w_out_e[j].astype(BF16),
    )


def _prep_odd(j, norm_pre_o, norm_post_o, w_in_o, sinks, w_out_o):
    D = D_MODEL
    w_in = w_in_o[j]
    row2 = lambda a: a.reshape(1, -1)
    return dict(
        gpre=row2(norm_pre_o[j]), gpost=row2(norm_post_o[j]),
        wqt=(w_in[:, :D].T * Q_SCALE_C).astype(BF16),
        wkv=w_in[:, D:D + 2 * LANES].astype(BF16),
        wz=w_in[:, D + 2 * LANES:].astype(BF16),
        wo=w_out_o[j].astype(BF16),
        sinks=jnp.pad(sinks[j], (0, LANES - H_C)).reshape(1, LANES),
    )


def _prep_odd_sample(j, w_in_o, sinks, w_out_o):
    D = D_MODEL
    w_in = w_in_o[j]
    on_group = (jnp.arange(H_C)[:, None] // G_C) == jnp.arange(N_KV_C)[None, :]

    def stack_cols(w):
        w4 = w.reshape(D, H_C, 1, HD_C)
        return jnp.where(on_group[None, :, :, None], w4, 0.0).reshape(D, H_C * LANES).astype(BF16)
    wo4 = w_out_o[j].reshape(H_C, 1, HD_C, D)
    return dict(
        wqs=stack_cols(w_in[:, :D]),
        wzs=stack_cols(w_in[:, D + 2 * LANES:]),
        wos=jnp.where(on_group[:, :, None, None], wo4, 0.0).reshape(H_C * LANES, D).astype(BF16),
        sinkcol=jnp.broadcast_to(sinks[j][:, None], (H_C, LANES)),
    )


def _rope_tables(pos):
    half = ROT_DIM // 2
    f32 = np.float32
    inv = np.power(f32(ROPE_THETA), -np.arange(half, dtype=f32) * f32(2.0 / ROT_DIM)).astype(f32)
    ang = np.asarray(pos, dtype=f32)[:, None] * inv[None, :]
    cos, sin = np.cos(ang).astype(f32), np.sin(ang).astype(f32)
    n = ang.shape[0]
    pad = HD_C - ROT_DIM
    cos64 = np.concatenate([cos, cos, np.ones((n, pad), f32)], axis=1)
    sin64 = np.concatenate([-sin, sin, np.zeros((n, pad), f32)], axis=1)
    ck = np.concatenate([cos64, cos64], axis=1)
    sk = np.concatenate([sin64, sin64], axis=1)
    tabs = dict(cq=ck * f32(Q_SCALE_C), sq=sk * f32(Q_SCALE_C), ck=ck, sk=sk,
                cos_t=np.ascontiguousarray(cos.T), sin_t=np.ascontiguousarray(sin.T))
    return {k: jnp.asarray(v) for k, v in tabs.items()}


def kernel(x_prompt, x_sample, state_conv_a, state_conv_b, state_mlstm_c, state_mlstm_n, state_mlstm_m, cache_k_win, cache_v_win, norm_pre_e, norm_post_e, w_in_e, conv_a_w, conv_a_b, ln_a_g, ln_a_b, conv_b_w, conv_b_b, wq_b, wk_b, b_i, b_f, headnorm_b, skip_b, w_out_e, norm_pre_o, norm_post_o, w_in_o, sinks, w_out_o):
    B, L, D = x_prompt.shape
    ew = _prep_even(0, norm_pre_e, norm_post_e, w_in_e, conv_a_w, conv_a_b, ln_a_g, ln_a_b, conv_b_w,
                    conv_b_b, wq_b, wk_b, b_i, b_f, headnorm_b, skip_b, w_out_e)
    ow = _prep_odd(0, norm_pre_o, norm_post_o, w_in_o, sinks, w_out_o)

    x1, ca_p, cb_p, c_p, n_p, m_p = _even_prompt(x_prompt, ew, min(TILE_EVEN, L))
    tp = _rope_tables(np.arange(L))
    y_p, kw_p, vw_p = _odd_prompt(x1, ow, (tp['cos_t'], tp['sin_t'], tp['ck'], tp['sk']), min(TILE_ODD, L))

    conv_a_p = ca_p[None, :, 32 - (CONV_A - 1):, :]
    conv_b_p = cb_p[None, :, 8 - (CONV_B - 1):, :]
    mlstm_c_p = c_p[None]
    mlstm_n_p = n_p[None]
    mlstm_m_p = m_p[None, :, :H_B, 0]
    k_win_p = kw_p.reshape(1, B, WINDOW, N_KV_C, HD_C)
    v_win_p = vw_p.reshape(1, B, WINDOW, N_KV_C, HD_C)

    N = x_sample.shape[0]
    osw = _prep_odd_sample(0, w_in_o, sinks, w_out_o)
    y_s, ca_s, cb_s, c_s, n_s, m_s, kw_s, vw_s = _sample_path(
        x_sample.reshape(N, D), state_conv_a, state_conv_b[0], state_mlstm_c[0], state_mlstm_n[0],
        state_mlstm_m[0], cache_k_win[0], cache_v_win[0], ew, ow, osw)
    return (y_p, y_s.reshape(N, 1, D), conv_a_p, conv_b_p, mlstm_c_p, mlstm_n_p, mlstm_m_p, k_win_p, v_win_p,
            ca_s, cb_s[None], c_s[None], n_s[None], m_s[None],
            kw_s.reshape(1, N, WINDOW, N_KV_C, HD_C), vw_s.reshape(1, N, WINDOW, N_KV_C, HD_C))
```
